```python
import jax, jax.numpy as jnp
from jax import lax
import numpy as np

D_MODEL = 1024
BATCH = 4
SEQ = 4096
DEPTH = 4

HEAD_DIM = 64
N_HEADS = D_MODEL // HEAD_DIM
N_HEADS_A = N_HEADS // 2
N_KV_A = N_HEADS_A // 4
N_HEADS_B = N_HEADS - N_HEADS_A
D_FF = 4 * D_MODEL
GRID_W = 64
AXIAL_THETA = 10000.0
ROPE_THETA = 500000.0
ROPE_DIM = HEAD_DIM // 4
BLOCK_Q = 128
DILATED_PATTERNS = ((128, 1), (512, 4), (2048, 16))
NORM_EPS = 1e-6
NEG_INF = -1e30

QA_W = N_HEADS_A * HEAD_DIM
KVA_W = N_KV_A * HEAD_DIM
QB_W = N_HEADS_B * HEAD_DIM
IN_W = QA_W + 2 * KVA_W + 3 * QB_W
MIX_W = QA_W + QB_W

kernel_name = "hybrid_gqa_axial_dilated_encoder"


def rmsnorm(x, g):
    xf = x.astype(jnp.float32)
    y = xf * lax.rsqrt(jnp.mean(xf * xf, axis=-1, keepdims=True) + NORM_EPS)
    return (y * g.astype(jnp.float32)).astype(x.dtype)


def rope(x, pos, theta):
    half = x.shape[-1] // 2
    freqs = theta ** (-jnp.arange(half, dtype=jnp.float32) / half)
    ang = pos.astype(jnp.float32)[:, None] * freqs[None, :]
    cos = jnp.cos(ang)[:, None, :]
    sin = jnp.sin(ang)[:, None, :]
    xf = x.astype(jnp.float32)
    x1, x2 = xf[..., :half], xf[..., half:]
    return jnp.concatenate([x1 * cos - x2 * sin, x2 * cos + x1 * sin], axis=-1).astype(x.dtype)


def axial_rope(x, row_ids, col_ids):
    h = x.shape[-1] // 2
    return jnp.concatenate([rope(x[..., :h], row_ids, AXIAL_THETA),
                            rope(x[..., h:], col_ids, AXIAL_THETA)], axis=-1)


def partial_rope(x, pos):
    return jnp.concatenate([rope(x[..., :ROPE_DIM], pos, ROPE_THETA), x[..., ROPE_DIM:]], axis=-1)


def global_gqa(q, k, v):
    B, S, HA, hd = q.shape
    HKV = k.shape[2]
    G = HA // HKV
    nb = S // BLOCK_Q
    scale = hd ** -0.5
    qb = q.reshape(B, nb, BLOCK_Q, HKV, G, hd).transpose(1, 0, 2, 3, 4, 5)

    def one(qblk):
        s = jnp.einsum('bqkgd,bskd->bkgqs', qblk, k).astype(jnp.float32) * scale
        p = jax.nn.softmax(s, axis=-1).astype(v.dtype)
        return jnp.einsum('bkgqs,bskd->bqkgd', p, v)

    o = lax.map(one, qb)
    return o.transpose(1, 0, 2, 3, 4, 5).reshape(B, S, HA * hd)


def dilated_window_attention(q, k, v):
    B, S, H, hd = q.shape
    nb = S // BLOCK_Q
    scale = hd ** -0.5
    qb = q.reshape(B, nb, BLOCK_Q, H, hd).transpose(1, 0, 2, 3, 4)
    starts = jnp.arange(nb, dtype=jnp.int32) * BLOCK_Q

    def one(args):
        qblk, start = args
        pos = start + jnp.arange(BLOCK_Q, dtype=jnp.int32)
        outs, lses = [], []
        for window, dil in DILATED_PATTERNS:
            n_side = window // (2 * dil)
            offs = jnp.arange(-n_side, n_side + 1, dtype=jnp.int32) * dil
            idx = pos[:, None] + offs[None, :]
            valid = (idx >= 0) & (idx < S)
            idxc = jnp.clip(idx, 0, S - 1)
            kg = k[:, idxc]
            vg = v[:, idxc]
            s = jnp.einsum('bqhd,bqjhd->bhqj', qblk, kg).astype(jnp.float32) * scale
            s = jnp.where(valid[None, None], s, NEG_INF)
            lse = jax.nn.logsumexp(s, axis=-1, keepdims=True)
            p = jnp.exp(s - lse).astype(v.dtype)
            outs.append(jnp.einsum('bhqj,bqjhd->bqhd', p, vg).astype(jnp.float32))
            lses.append(lse[..., 0])
        wts = jax.nn.softmax(jnp.stack(lses, axis=0), axis=0)
        o = jnp.einsum('pbhq,pbqhd->bqhd', wts, jnp.stack(outs, axis=0))
        return o.astype(v.dtype)

    o = lax.map(one, (qb, starts))
    return o.transpose(1, 0, 2, 3, 4).reshape(B, S, H * hd)


def hybrid_layer(x, n1, w_in, q_norm, k_norm, out_norm_a, out_norm_b, w_out,
                 n2, w_mlp_in, w_mlp_out, row_ids, col_ids, pos):
    B, S, _ = x.shape
    h = rmsnorm(x, n1)
    proj = h @ w_in
    cuts = np.cumsum([QA_W, KVA_W, KVA_W, QB_W, QB_W]).tolist()
    qa, ka, va, qb, kb, vb = jnp.split(proj, cuts, axis=-1)

    qa = qa.reshape(B, S, N_HEADS_A, HEAD_DIM)
    ka = ka.reshape(B, S, N_KV_A, HEAD_DIM)
    va = va.reshape(B, S, N_KV_A, HEAD_DIM)
    qa = axial_rope(rmsnorm(qa, q_norm), row_ids, col_ids)
    ka = axial_rope(rmsnorm(ka, k_norm), row_ids, col_ids)
    ya = global_gqa(qa, ka, va)

    qb = partial_rope(qb.reshape(B, S, N_HEADS_B, HEAD_DIM), pos)
    kb = partial_rope(kb.reshape(B, S, N_HEADS_B, HEAD_DIM), pos)
    vb = vb.reshape(B, S, N_HEADS_B, HEAD_DIM)
    yb = dilated_window_attention(qb, kb, vb)

    y = jnp.concatenate([rmsnorm(ya, out_norm_a), rmsnorm(yb, out_norm_b)], axis=-1)
    x = x + y @ w_out

    u = rmsnorm(x, n2) @ w_mlp_in
    u = jnp.square(jax.nn.relu(u))
    return x + u @ w_mlp_out


def setup_inputs(seed: int = 0) -> dict:
    key = jax.random.key(seed)
    ks = jax.random.split(key, 13)
    f32 = jnp.float32

    def gain(k, shape):
        return 1.0 + 0.02 * jax.random.normal(k, shape, f32)

    return {
        "x": jax.random.normal(ks[0], (BATCH, SEQ, D_MODEL), f32),
        "norm1": gain(ks[1], (DEPTH, D_MODEL)),
        "w_in": jax.random.normal(ks[2], (DEPTH, D_MODEL, IN_W), f32) * D_MODEL ** -0.5,
        "q_norm": gain(ks[3], (DEPTH, HEAD_DIM)),
        "k_norm": gain(ks[4], (DEPTH, HEAD_DIM)),
        "out_norm_a": gain(ks[5], (DEPTH, QA_W)),
        "out_norm_b": gain(ks[6], (DEPTH, QB_W)),
        "w_out": jax.random.normal(ks[7], (DEPTH, MIX_W, D_MODEL), f32) * MIX_W ** -0.5,
        "norm2": gain(ks[8], (DEPTH, D_MODEL)),
        "w_mlp_in": jax.random.normal(ks[9], (DEPTH, D_MODEL, D_FF), f32) * D_MODEL ** -0.5,
        "w_mlp_out": jax.random.normal(ks[10], (DEPTH, D_FF, D_MODEL), f32) * D_FF ** -0.5,
        "final_norm": gain(ks[11], (D_MODEL,)),
    }


def reference(x, norm1, w_in, q_norm, k_norm, out_norm_a, out_norm_b, w_out,
              norm2, w_mlp_in, w_mlp_out, final_norm):
    S = x.shape[1]
    rows = S // GRID_W
    row_ids = jnp.repeat(jnp.arange(rows, dtype=jnp.int32), GRID_W)
    col_ids = jnp.tile(jnp.arange(GRID_W, dtype=jnp.int32), rows)
    pos = jnp.arange(S, dtype=jnp.int32)
    for l in range(DEPTH):
        x = hybrid_layer(x, norm1[l], w_in[l], q_norm[l], k_norm[l],
                         out_norm_a[l], out_norm_b[l], w_out[l], norm2[l],
                         w_mlp_in[l], w_mlp_out[l], row_ids, col_ids, pos)
    return rmsnorm(x, final_norm)
```

```python
import functools

import jax
import jax.numpy as jnp
import numpy as np
from jax import lax
from jax.experimental import pallas as pl
from jax.experimental.pallas import tpu as pltpu

D_MODEL = 1024
HEAD_DIM = 64
N_HEADS_A = 8
N_KV_A = 2
N_HEADS_B = 8
D_FF = 4 * D_MODEL
GRID_W = 64
AXIAL_THETA = 10000.0
ROPE_THETA = 500000.0
ROPE_DIM = HEAD_DIM // 4
DILATED_PATTERNS = ((128, 1), (512, 4), (2048, 16))
NORM_EPS = 1e-6
NEG_INF = -1e30

QA_W = N_HEADS_A * HEAD_DIM
KVA_W = N_KV_A * HEAD_DIM
QB_W = N_HEADS_B * HEAD_DIM
IN_W = QA_W + 2 * KVA_W + 3 * QB_W
SCALE = HEAD_DIM ** -0.5

LANES = 128
VMEM_LIMIT = 56 * 1024 * 1024

_F32 = jnp.float32
_BF16 = jnp.bfloat16


def _cparams(n_grid):
    return pltpu.CompilerParams(
        dimension_semantics=("arbitrary",) * n_grid, vmem_limit_bytes=VMEM_LIMIT)


def _dot(a, b):
    return jnp.dot(a, b, preferred_element_type=_F32)


def _dot_nt(a, b):
    return lax.dot_general(a, b, (((1,), (1,)), ((), ())), preferred_element_type=_F32)


def _rms_scale(x):
    return lax.rsqrt(jnp.mean(x * x, axis=-1, keepdims=True) + NORM_EPS)


def _rope_tables(seq):
    lane = np.arange(LANES)
    e = lane % HEAD_DIM
    t = jnp.arange(seq, dtype=jnp.int32)
    half = HEAD_DIM // 4
    freqs = AXIAL_THETA ** (-jnp.arange(half, dtype=_F32) / half)
    f_idx = e % half
    pos = jnp.where((e < HEAD_DIM // 2)[None, :], (t // GRID_W)[:, None], (t % GRID_W)[:, None])
    ang = pos.astype(_F32) * freqs[f_idx][None, :]
    sign = np.where((e % (2 * half)) < half, -1.0, 1.0).astype(np.float32)
    ax_cos = jnp.cos(ang)
    ax_sin = jnp.sin(ang) * sign[None, :]
    halfp = ROPE_DIM // 2
    freqs_p = ROPE_THETA ** (-jnp.arange(halfp, dtype=_F32) / halfp)
    angp = t.astype(_F32)[:, None] * freqs_p[e % halfp][None, :]
    rot = (e < ROPE_DIM)[None, :]
    signp = np.where(e < halfp, -1.0, 1.0).astype(np.float32)
    pr_cos = jnp.where(rot, jnp.cos(angp), 1.0)
    pr_sin = jnp.where(rot, jnp.sin(angp) * signp[None, :], 0.0)
    return ax_cos, ax_sin, pr_cos, pr_sin


def _group_mean_matrix():
    g = np.arange(LANES) // HEAD_DIM
    return jnp.asarray((g[:, None] == g[None, :]).astype(np.float32) / HEAD_DIM, dtype=_BF16)


def _in_kernel(x_ref, n1_ref, w_ref, qg_ref, kg_ref, gm_ref,
               axc_ref, axs_ref, prc_ref, prs_ref,
               qa_ref, ka_ref, va_ref, qb_ref, kb_ref, vb_ref):
    x = x_ref[0]
    h = (x * _rms_scale(x) * n1_ref[...]).astype(_BF16)
    tm = x.shape[0]
    lane = lax.broadcasted_iota(jnp.int32, (tm, LANES), 1)
    e = lane % HEAD_DIM
    low = lane < HEAD_DIM
    gm = gm_ref[...]

    def head_norm(v, gain):
        sq = v * v
        hi = sq.astype(_BF16)
        lo = (sq - hi.astype(_F32)).astype(_BF16)
        ms = _dot(hi, gm) + _dot(lo, gm)
        return v * lax.rsqrt(ms + NORM_EPS) * gain

    def axial(v):
        first = (e % 32) < 16
        partner = jnp.where(first, pltpu.roll(v, LANES - 16, 1), pltpu.roll(v, 16, 1))
        return v * axc_ref[...] + partner * axs_ref[...]

    def partial(v):
        partner = jnp.where(e < 8, pltpu.roll(v, LANES - 8, 1), pltpu.roll(v, 8, 1))
        return v * prc_ref[...] + partner * prs_ref[...]

    pa = _dot(h, w_ref[:, 0:QA_W])
    for c in range(QA_W // LANES):
        v = axial(head_norm(pa[:, c * LANES:(c + 1) * LANES], qg_ref[...])) * SCALE
        swapped = pltpu.roll(v, HEAD_DIM, 1)
        for j in range(2):
            hd = 2 * c + j
            grp = hd // (N_HEADS_A // N_KV_A)
            src = v if j == grp else swapped
            keep = low if grp == 0 else jnp.logical_not(low)
            qa_ref[0, hd] = jnp.where(keep, src, 0.0).astype(_BF16)

    pkv = _dot(h, w_ref[:, QA_W:QA_W + 2 * KVA_W])
    ka_ref[0] = axial(head_norm(pkv[:, 0:LANES], kg_ref[...])).astype(_BF16)
    va_ref[0] = pkv[:, LANES:2 * LANES].astype(_BF16)

    o = QA_W + 2 * KVA_W
    pq = _dot(h, w_ref[:, o:o + QB_W])
    pk = _dot(h, w_ref[:, o + QB_W:o + 2 * QB_W])
    for c in range(QB_W // LANES):
        sl = slice(c * LANES, (c + 1) * LANES)
        qb_ref[0, :, sl] = (partial(pq[:, sl]) * SCALE).astype(_BF16)
        kb_ref[0, :, sl] = partial(pk[:, sl]).astype(_BF16)
    vb_ref[0] = _dot(h, w_ref[:, o + 2 * QB_W:o + 3 * QB_W]).astype(_BF16)


def _in_proj(x, n1, w_in, qg, kg, gm, tables, tm):
    B, S, D = x.shape
    nt = S // tm
    const = lambda b, i: (0, 0)
    tab = pl.BlockSpec((tm, LANES), lambda b, i: (i, 0))
    out_shape = (
        jax.ShapeDtypeStruct((B, N_HEADS_A, S, LANES), _BF16),
        jax.ShapeDtypeStruct((B, S, LANES), _BF16),
        jax.ShapeDtypeStruct((B, S, LANES), _BF16),
        jax.ShapeDtypeStruct((B, S, QB_W), _BF16),
        jax.ShapeDtypeStruct((B, S, QB_W), _BF16),
        jax.ShapeDtypeStruct((B, S, QB_W), _BF16),
    )
    wide = pl.BlockSpec((1, tm, QB_W), lambda b, i: (b, i, 0))
    narrow = pl.BlockSpec((1, tm, LANES), lambda b, i: (b, i, 0))
    return pl.pallas_call(
        _in_kernel,
        grid=(B, nt),
        in_specs=[
            pl.BlockSpec((1, tm, D), lambda b, i: (b, i, 0)),
            pl.BlockSpec((1, D), const),
            pl.BlockSpec((D, IN_W), const, pipeline_mode=pl.Buffered(1)),
            pl.BlockSpec((1, LANES), const),
            pl.BlockSpec((1, LANES), const),
            pl.BlockSpec((LANES, LANES), const),
            tab, tab, tab, tab,
        ],
        out_specs=(
            pl.BlockSpec((1, N_HEADS_A, tm, LANES), lambda b, i: (b, 0, i, 0)),
            narrow, narrow, wide, wide, wide,
        ),
        out_shape=out_shape,
        compiler_params=_cparams(2),
        name="in_proj",
    )(x, n1, w_in, qg, kg, gm, *tables)


def _gqa_kernel(q_ref, k_ref, v_ref, o_ref, m_ref, l_ref, acc_ref, *, tk):
    g = pl.program_id(1)
    n_q, tq = q_ref.shape[1], q_ref.shape[2]
    rows = n_q * tq
    q = q_ref[0].reshape(rows, LANES)
    m_ref[...] = jnp.full(m_ref.shape, NEG_INF, _F32)
    l_ref[...] = jnp.zeros(l_ref.shape, _F32)
    acc_ref[...] = jnp.zeros(acc_ref.shape, _F32)
    n_kv = k_ref.shape[1] // tk

    def body(i, carry):
        start = pl.multiple_of(i * tk, tk)
        k = k_ref[0, pl.ds(start, tk), :]
        v = v_ref[0, pl.ds(start, tk), :]
        s = _dot_nt(q, k)
        m_prev = m_ref[...]
        m_new = jnp.maximum(m_prev, jnp.max(s, axis=-1, keepdims=True))
        alpha = jnp.exp(m_prev - m_new)
        p = jnp.exp(s - m_new[:, 0:1])
        l_ref[...] = alpha * l_ref[...] + jnp.sum(p, axis=-1, keepdims=True)
        acc_ref[...] = alpha * acc_ref[...] + _dot(p.astype(_BF16), v)
        m_ref[...] = m_new
        return carry

    lax.fori_loop(0, n_kv, body, 0)
    out = acc_ref[...] / l_ref[...]
    out = jnp.where(g == 0, out, pltpu.roll(out, HEAD_DIM, 1))
    for j in range(n_q // 2):
        a = out[(2 * j) * tq:(2 * j + 1) * tq]
        b = pltpu.roll(out[(2 * j + 1) * tq:(2 * j + 2) * tq], HEAD_DIM, 1)
        lane = lax.broadcasted_iota(jnp.int32, a.shape, 1)
        o_ref[0, :, j * LANES:(j + 1) * LANES] = jnp.where(lane < HEAD_DIM, a, b)


def _gqa(qa, ka, va, tq, tk):
    B, H, S, _ = qa.shape
    per = H // N_KV_A
    rows = per * tq
    return pl.pallas_call(
        functools.partial(_gqa_kernel, tk=tk),
        grid=(B, N_KV_A, S // tq),
        in_specs=[
            pl.BlockSpec((1, per, tq, LANES), lambda b, g, i: (b, g, i, 0)),
            pl.BlockSpec((1, S, LANES), lambda b, g, i: (b, 0, 0)),
            pl.BlockSpec((1, S, LANES), lambda b, g, i: (b, 0, 0)),
        ],
        out_specs=pl.BlockSpec((1, tq, per * HEAD_DIM), lambda b, g, i: (b, i, g)),
        out_shape=jax.ShapeDtypeStruct((B, S, QA_W), _F32),
        scratch_shapes=[pltpu.VMEM((rows, LANES), _F32)] * 3,
        compiler_params=_cparams(3),
        name="gqa_attn",
    )(qa, ka, va)


def _dil_kernel(q_ref, k_ref, v_ref, o_ref, lse_ref, *, n_side, win):
    i = pl.program_id(2)
    tq = q_ref.shape[1]
    sub_len = k_ref.shape[1]
    q0 = i * tq
    start = jnp.clip(q0 - n_side, 0, sub_len - win)
    start = pl.multiple_of(start, n_side)
    qpos = q0 + lax.broadcasted_iota(jnp.int32, (tq, win), 0)
    kpos = start + lax.broadcasted_iota(jnp.int32, (tq, win), 1)
    valid = jnp.abs(kpos - qpos) <= n_side
    lane = lax.broadcasted_iota(jnp.int32, (tq, LANES), 1)
    low = lane < HEAD_DIM
    for c in range(QB_W // LANES):
        sl = slice(c * LANES, (c + 1) * LANES)
        qc = q_ref[0, :, sl]
        kc = k_ref[0, pl.ds(start, win), sl]
        vc = v_ref[0, pl.ds(start, win), sl]
        outs, lses = [], []
        for keep in (low, jnp.logical_not(low)):
            qh = jnp.where(keep, qc, jnp.zeros_like(qc))
            s = jnp.where(valid, _dot_nt(qh, kc), NEG_INF)
            m = jnp.max(s, axis=-1, keepdims=True)
            p = jnp.exp(s - m)
            l = jnp.sum(p, axis=-1, keepdims=True)
            outs.append(_dot(p.astype(_BF16), vc) / l)
            lses.append(jnp.broadcast_to(m + jnp.log(l), (tq, LANES)))
        o_ref[0, :, sl] = jnp.where(low, outs[0], outs[1])
        lse_ref[0, :, sl] = jnp.where(low, lses[0], lses[1])


def _dilated(qb, kb, vb, window, dil, tq):
    B, S, W = qb.shape
    n_side = window // (2 * dil)
    sub_len = S // dil
    tq = min(tq, sub_len)
    win = min(tq + 2 * n_side, sub_len)
    view = lambda a: a.reshape(B, sub_len, dil * W)
    qspec = pl.BlockSpec((1, tq, W), lambda b, r, i: (b, i, r))
    kspec = pl.BlockSpec((1, sub_len, W), lambda b, r, i: (b, 0, r))
    o, lse = pl.pallas_call(
        functools.partial(_dil_kernel, n_side=n_side, win=win),
        grid=(B, dil, sub_len // tq),
        in_specs=[qspec, kspec, kspec],
        out_specs=(qspec, qspec),
        out_shape=(jax.ShapeDtypeStruct((B, sub_len, dil * W), _F32),) * 2,
        compiler_params=_cparams(3),
        name=f"dilated_d{dil}",
    )(view(qb), view(kb), view(vb))
    return o.reshape(B, S, W), lse.reshape(B, S, W)


def _out_kernel(x_ref, ya_ref, o1_ref, o2_ref, o3_ref, l1_ref, l2_ref, l3_ref,
                ga_ref, gb_ref, w_ref, out_ref):
    l1, l2, l3 = l1_ref[0], l2_ref[0], l3_ref[0]
    mx = jnp.maximum(jnp.maximum(l1, l2), l3)
    e1, e2, e3 = jnp.exp(l1 - mx), jnp.exp(l2 - mx), jnp.exp(l3 - mx)
    yb = (e1 * o1_ref[0] + e2 * o2_ref[0] + e3 * o3_ref[0]) / (e1 + e2 + e3)
    ya = ya_ref[0]
    ya_n = (ya * _rms_scale(ya) * ga_ref[...]).astype(_BF16)
    yb_n = (yb * _rms_scale(yb) * gb_ref[...]).astype(_BF16)
    y = _dot(ya_n, w_ref[0:QA_W, :]) + _dot(yb_n, w_ref[QA_W:QA_W + QB_W, :])
    out_ref[0] = x_ref[0] + y


def _out_proj(x, ya, os_, lses, ga, gb, w_out, tm):
    B, S, D = x.shape
    const = lambda b, i: (0, 0)
    half = pl.BlockSpec((1, tm, QA_W), lambda b, i: (b, i, 0))
    full = pl.BlockSpec((1, tm, D), lambda b, i: (b, i, 0))
    return pl.pallas_call(
        _out_kernel,
        grid=(B, S // tm),
        in_specs=[full, half, half, half, half, half, half, half,
                  pl.BlockSpec((1, QA_W), const), pl.BlockSpec((1, QB_W), const),
                  pl.BlockSpec((QA_W + QB_W, D), const, pipeline_mode=pl.Buffered(1))],
        out_specs=full,
        out_shape=jax.ShapeDtypeStruct((B, S, D), _F32),
        compiler_params=_cparams(2),
        name="out_proj",
    )(x, ya, *os_, *lses, ga, gb, w_out)


def _mlp_kernel(x_ref, n2_ref, w1_ref, w2_ref, fn_ref, out_ref, *, chunk, final_norm):
    x = x_ref[0]
    h = (x * _rms_scale(x) * n2_ref[...]).astype(_BF16)
    acc = x
    for c in range(D_FF // chunk):
        sl = slice(c * chunk, (c + 1) * chunk)
        u = jnp.maximum(_dot(h, w1_ref[:, sl]), 0.0)
        acc = acc + _dot((u * u).astype(_BF16), w2_ref[sl, :])
    if final_norm:
        acc = acc * _rms_scale(acc) * fn_ref[...]
    out_ref[0] = acc


def _mlp(x, n2, w1, w2, fn, tm, final_norm):
    B, S, D = x.shape
    const = lambda b, i: (0, 0)
    full = pl.BlockSpec((1, tm, D), lambda b, i: (b, i, 0))
    return pl.pallas_call(
        functools.partial(_mlp_kernel, chunk=1024, final_norm=final_norm),
        grid=(B, S // tm),
        in_specs=[full, pl.BlockSpec((1, D), const),
                  pl.BlockSpec((D, D_FF), const, pipeline_mode=pl.Buffered(1)),
                  pl.BlockSpec((D_FF, D), const, pipeline_mode=pl.Buffered(1)),
                  pl.BlockSpec((1, D), const)],
        out_specs=full,
        out_shape=jax.ShapeDtypeStruct((B, S, D), _F32),
        compiler_params=_cparams(2),
        name="mlp",
    )(x, n2, w1, w2, fn)


def kernel(x, norm1, w_in, q_norm, k_norm, out_norm_a, out_norm_b, w_out,
           norm2, w_mlp_in, w_mlp_out, final_norm):
    B, S, D = x.shape
    depth = w_in.shape[0]
    tables = _rope_tables(S)
    gm = _group_mean_matrix()
    row = lambda a: a.reshape(1, -1).astype(_F32)
    fn = row(final_norm)
    for l in range(depth):
        qg = row(jnp.tile(q_norm[l], LANES // HEAD_DIM))
        kg = row(jnp.tile(k_norm[l], LANES // HEAD_DIM))
        qa, ka, va, qb, kb, vb = _in_proj(
            x, row(norm1[l]), w_in[l].astype(_BF16), qg, kg, gm, tables, tm=512)
        ya = _gqa(qa, ka, va, tq=256, tk=512)
        os_, lses = [], []
        for window, dil in DILATED_PATTERNS:
            o, lse = _dilated(qb, kb, vb, window, dil, tq=256)
            os_.append(o)
            lses.append(lse)
        x = _out_proj(x, ya, os_, lses, row(out_norm_a[l]), row(out_norm_b[l]),
                      w_out[l].astype(_BF16), tm=512)
        x = _mlp(x, row(norm2[l]), w_mlp_in[l].astype(_BF16), w_mlp_out[l].astype(_BF16),
                 fn, tm=512, final_norm=(l == depth - 1))
    return x
```

```python
import functools

import jax
import jax.numpy as jnp
import numpy as np
from jax import lax
from jax.experimental import pallas as pl
from jax.experimental.pallas import tpu as pltpu

D_MODEL = 1024
HEAD_DIM = 64
N_HEADS_A = 8
N_KV_A = 2
N_HEADS_B = 8
D_FF = 4 * D_MODEL
GRID_W = 64
AXIAL_THETA = 10000.0
ROPE_THETA = 500000.0
ROPE_DIM = HEAD_DIM // 4
DILATED_PATTERNS = ((128, 1), (512, 4), (2048, 16))
NORM_EPS = 1e-6
NEG_INF = -1e30

QA_W = N_HEADS_A * HEAD_DIM
KVA_W = N_KV_A * HEAD_DIM
QB_W = N_HEADS_B * HEAD_DIM
IN_W = QA_W + 2 * KVA_W + 3 * QB_W
SCALE = HEAD_DIM ** -0.5
SCALE_A = SCALE * float(np.log2(np.e))

LANES = 128
VMEM_LIMIT = 56 * 1024 * 1024

_F32 = jnp.float32
_BF16 = jnp.bfloat16


def _cparams(n_grid):
    return pltpu.CompilerParams(
        dimension_semantics=("arbitrary",) * n_grid, vmem_limit_bytes=VMEM_LIMIT)


def _dot(a, b):
    return jnp.dot(a, b, preferred_element_type=_F32)


def _dot_nt(a, b):
    return lax.dot_general(a, b, (((1,), (1,)), ((), ())), preferred_element_type=_F32)


def _rms_scale(x):
    return lax.rsqrt(jnp.mean(x * x, axis=-1, keepdims=True) + NORM_EPS)


def _rope_tables(seq):
    lane = np.arange(LANES)
    e = lane % HEAD_DIM
    t = jnp.arange(seq, dtype=jnp.int32)
    half = HEAD_DIM // 4
    freqs = AXIAL_THETA ** (-jnp.arange(half, dtype=_F32) / half)
    f_idx = e % half
    pos = jnp.where((e < HEAD_DIM // 2)[None, :], (t // GRID_W)[:, None], (t % GRID_W)[:, None])
    ang = pos.astype(_F32) * freqs[f_idx][None, :]
    sign = np.where((e % (2 * half)) < half, -1.0, 1.0).astype(np.float32)
    ax_cos = jnp.cos(ang)
    ax_sin = jnp.sin(ang) * sign[None, :]
    halfp = ROPE_DIM // 2
    freqs_p = ROPE_THETA ** (-jnp.arange(halfp, dtype=_F32) / halfp)
    angp = t.astype(_F32)[:, None] * freqs_p[e % halfp][None, :]
    rot = (e < ROPE_DIM)[None, :]
    signp = np.where(e < halfp, -1.0, 1.0).astype(np.float32)
    pr_cos = jnp.where(rot, jnp.cos(angp), 1.0)
    pr_sin = jnp.where(rot, jnp.sin(angp) * signp[None, :], 0.0)
    return ax_cos, ax_sin, pr_cos, pr_sin


def _group_mean_matrix():
    g = np.arange(LANES) // HEAD_DIM
    return jnp.asarray((g[:, None] == g[None, :]).astype(np.float32) / HEAD_DIM, dtype=_BF16)


def _in_kernel(x_ref, n1_ref, w_ref, qg_ref, kg_ref, gm_ref,
               axc_ref, axs_ref, prc_ref, prs_ref,
               qa_ref, ka_ref, vt_ref, qb_ref, kb_ref, vb_ref):
    x = x_ref[0]
    h = (x * _rms_scale(x) * n1_ref[...]).astype(_BF16)
    tm = x.shape[0]
    lane = lax.broadcasted_iota(jnp.int32, (tm, LANES), 1)
    e = lane % HEAD_DIM
    low = lane < HEAD_DIM
    gm = gm_ref[...]

    def head_norm(v, gain):
        sq = v * v
        hi = sq.astype(_BF16)
        lo = (sq - hi.astype(_F32)).astype(_BF16)
        ms = _dot(hi, gm) + _dot(lo, gm)
        return v * lax.rsqrt(ms + NORM_EPS) * gain

    def axial(v):
        first = (e % 32) < 16
        partner = jnp.where(first, pltpu.roll(v, LANES - 16, 1), pltpu.roll(v, 16, 1))
        return v * axc_ref[...] + partner * axs_ref[...]

    def partial(v):
        partner = jnp.where(e < 8, pltpu.roll(v, LANES - 8, 1), pltpu.roll(v, 8, 1))
        return v * prc_ref[...] + partner * prs_ref[...]

    pa = _dot(h, w_ref[:, 0:QA_W])
    for c in range(QA_W // LANES):
        v = axial(head_norm(pa[:, c * LANES:(c + 1) * LANES], qg_ref[...])) * SCALE_A
        swapped = pltpu.roll(v, HEAD_DIM, 1)
        for j in range(2):
            hd = 2 * c + j
            grp = hd // (N_HEADS_A // N_KV_A)
            src = v if j == grp else swapped
            keep = low if grp == 0 else jnp.logical_not(low)
            qa_ref[0, hd] = jnp.where(keep, src, 0.0).astype(_BF16)

    pkv = _dot(h, w_ref[:, QA_W:QA_W + 2 * KVA_W])
    ka_ref[0] = axial(head_norm(pkv[:, 0:LANES], kg_ref[...])).astype(_BF16)
    vt = pkv[:, LANES:2 * LANES].T
    ones = jnp.ones((HEAD_DIM, tm), _F32)
    for g in range(N_KV_A):
        vg = vt[g * HEAD_DIM:(g + 1) * HEAD_DIM]
        vt_ref[0, g, 0] = jnp.concatenate([vg, ones], axis=0).astype(_BF16)

    o = QA_W + 2 * KVA_W
    pq = _dot(h, w_ref[:, o:o + QB_W])
    pk = _dot(h, w_ref[:, o + QB_W:o + 2 * QB_W])
    for c in range(QB_W // LANES):
        sl = slice(c * LANES, (c + 1) * LANES)
        qb_ref[0, :, sl] = (partial(pq[:, sl]) * SCALE).astype(_BF16)
        kb_ref[0, :, sl] = partial(pk[:, sl]).astype(_BF16)
    vb_ref[0] = _dot(h, w_ref[:, o + 2 * QB_W:o + 3 * QB_W]).astype(_BF16)


def _in_proj(x, n1, w_in, qg, kg, gm, tables, tm):
    B, S, D = x.shape
    nt = S // tm
    const = lambda b, i: (0, 0)
    tab = pl.BlockSpec((tm, LANES), lambda b, i: (i, 0))
    out_shape = (
        jax.ShapeDtypeStruct((B, N_HEADS_A, S, LANES), _BF16),
        jax.ShapeDtypeStruct((B, S, LANES), _BF16),
        jax.ShapeDtypeStruct((B, N_KV_A, nt, LANES, tm), _BF16),
        jax.ShapeDtypeStruct((B, S, QB_W), _BF16),
        jax.ShapeDtypeStruct((B, S, QB_W), _BF16),
        jax.ShapeDtypeStruct((B, S, QB_W), _BF16),
    )
    wide = pl.BlockSpec((1, tm, QB_W), lambda b, i: (b, i, 0))
    narrow = pl.BlockSpec((1, tm, LANES), lambda b, i: (b, i, 0))
    return pl.pallas_call(
        _in_kernel,
        grid=(B, nt),
        in_specs=[
            pl.BlockSpec((1, tm, D), lambda b, i: (b, i, 0)),
            pl.BlockSpec((1, D), const),
            pl.BlockSpec((D, IN_W), const, pipeline_mode=pl.Buffered(1)),
            pl.BlockSpec((1, LANES), const),
            pl.BlockSpec((1, LANES), const),
            pl.BlockSpec((LANES, LANES), const),
            tab, tab, tab, tab,
        ],
        out_specs=(
            pl.BlockSpec((1, N_HEADS_A, tm, LANES), lambda b, i: (b, 0, i, 0)),
            narrow,
            pl.BlockSpec((1, N_KV_A, 1, LANES, tm), lambda b, i: (b, 0, i, 0, 0)),
            wide, wide, wide,
        ),
        out_shape=out_shape,
        compiler_params=_cparams(2),
        name="in_proj",
    )(x, n1, w_in, qg, kg, gm, *tables)


def _gqa_kernel(q_ref, k_ref, vt_ref, o_ref, acc_ref, st_ref):
    n_q, tq = q_ref.shape[1], q_ref.shape[2]
    cols = n_q * tq
    n_kv, tk = vt_ref.shape[2], vt_ref.shape[4]
    q = q_ref[0].reshape(cols, LANES)
    acc_ref[...] = jnp.zeros(acc_ref.shape, _F32)

    def scores(i):
        return _dot_nt(k_ref[0, i * tk:(i + 1) * tk, :], q)

    m_prev = jnp.full((1, cols), NEG_INF, _F32)
    st_ref[0] = scores(0)
    for i in range(n_kv):
        if i + 1 < n_kv:
            st_ref[(i + 1) % 2] = scores(i + 1)
        st = st_ref[i % 2]
        m_new = jnp.maximum(m_prev, jnp.max(st, axis=0, keepdims=True))
        alpha = jnp.exp2(m_prev - m_new)
        p = jnp.exp2(st - m_new).astype(_BF16)
        acc_ref[...] = alpha * acc_ref[...] + _dot(vt_ref[0, 0, i], p)
        m_prev = m_new
    acc = acc_ref[...]
    out_t = acc[0:HEAD_DIM] / acc[HEAD_DIM:2 * HEAD_DIM]
    y_t = jnp.concatenate([out_t[:, j * tq:(j + 1) * tq] for j in range(n_q)], axis=0)
    o_ref[0] = y_t.T


def _gqa(qa, ka, vt, tq):
    B, H, S, _ = qa.shape
    per = H // N_KV_A
    n_kv, tk = vt.shape[2], vt.shape[4]
    return pl.pallas_call(
        _gqa_kernel,
        grid=(B, N_KV_A, S // tq),
        in_specs=[
            pl.BlockSpec((1, per, tq, LANES), lambda b, g, i: (b, g, i, 0)),
            pl.BlockSpec((1, S, LANES), lambda b, g, i: (b, 0, 0)),
            pl.BlockSpec((1, 1, n_kv, LANES, tk), lambda b, g, i: (b, g, 0, 0, 0)),
        ],
        out_specs=pl.BlockSpec((1, tq, per * HEAD_DIM), lambda b, g, i: (b, i, g)),
        out_shape=jax.ShapeDtypeStruct((B, S, QA_W), _F32),
        scratch_shapes=[pltpu.VMEM((LANES, per * tq), _F32),
                        pltpu.VMEM((2, tk, per * tq), _F32)],
        compiler_params=_cparams(3),
        name="gqa_attn",
    )(qa, ka, vt)


def _dil_kernel(q_ref, k_ref, v_ref, o_ref, lse_ref, *, n_side, win):
    i = pl.program_id(2)
    tq = q_ref.shape[1]
    sub_len = k_ref.shape[1]
    q0 = i * tq
    start = jnp.clip(q0 - n_side, 0, sub_len - win)
    start = pl.multiple_of(start, n_side)
    qpos = q0 + lax.broadcasted_iota(jnp.int32, (tq, win), 0)
    kpos = start + lax.broadcasted_iota(jnp.int32, (tq, win), 1)
    valid = jnp.abs(kpos - qpos) <= n_side
    lane = lax.broadcasted_iota(jnp.int32, (tq, LANES), 1)
    low = lane < HEAD_DIM
    for c in range(QB_W // LANES):
        sl = slice(c * LANES, (c + 1) * LANES)
        qc = q_ref[0, :, sl]
        kc = k_ref[0, pl.ds(start, win), sl]
        vc = v_ref[0, pl.ds(start, win), sl]
        outs, lses = [], []
        for keep in (low, jnp.logical_not(low)):
            qh = jnp.where(keep, qc, jnp.zeros_like(qc))
            s = jnp.where(valid, _dot_nt(qh, kc), NEG_INF)
            m = jnp.max(s, axis=-1, keepdims=True)
            p = jnp.exp(s - m)
            l = jnp.sum(p, axis=-1, keepdims=True)
            outs.append(_dot(p.astype(_BF16), vc) / l)
            lses.append(jnp.broadcast_to(m + jnp.log(l), (tq, LANES)))
        o_ref[0, :, sl] = jnp.where(low, outs[0], outs[1])
        lse_ref[0, :, sl] = jnp.where(low, lses[0], lses[1])


def _dilated(qb, kb, vb, window, dil, tq):
    B, S, W = qb.shape
    n_side = window // (2 * dil)
    sub_len = S // dil
    tq = min(tq, sub_len)
    win = min(tq + 2 * n_side, sub_len)
    view = lambda a: a.reshape(B, sub_len, dil * W)
    qspec = pl.BlockSpec((1, tq, W), lambda b, r, i: (b, i, r))
    kspec = pl.BlockSpec((1, sub_len, W), lambda b, r, i: (b, 0, r))
    o, lse = pl.pallas_call(
        functools.partial(_dil_kernel, n_side=n_side, win=win),
        grid=(B, dil, sub_len // tq),
        in_specs=[qspec, kspec, kspec],
        out_specs=(qspec, qspec),
        out_shape=(jax.ShapeDtypeStruct((B, sub_len, dil * W), _F32),) * 2,
        compiler_params=_cparams(3),
        name=f"dilated_d{dil}",
    )(view(qb), view(kb), view(vb))
    return o.reshape(B, S, W), lse.reshape(B, S, W)


def _out_kernel(x_ref, ya_ref, o1_ref, o2_ref, o3_ref, l1_ref, l2_ref, l3_ref,
                ga_ref, gb_ref, w_ref, out_ref):
    l1, l2, l3 = l1_ref[0], l2_ref[0], l3_ref[0]
    mx = jnp.maximum(jnp.maximum(l1, l2), l3)
    e1, e2, e3 = jnp.exp(l1 - mx), jnp.exp(l2 - mx), jnp.exp(l3 - mx)
    yb = (e1 * o1_ref[0] + e2 * o2_ref[0] + e3 * o3_ref[0]) / (e1 + e2 + e3)
    ya = ya_ref[0]
    ya_n = (ya * _rms_scale(ya) * ga_ref[...]).astype(_BF16)
    yb_n = (yb * _rms_scale(yb) * gb_ref[...]).astype(_BF16)
    y = _dot(ya_n, w_ref[0:QA_W, :]) + _dot(yb_n, w_ref[QA_W:QA_W + QB_W, :])
    out_ref[0] = x_ref[0] + y


def _out_proj(x, ya, os_, lses, ga, gb, w_out, tm):
    B, S, D = x.shape
    const = lambda b, i: (0, 0)
    half = pl.BlockSpec((1, tm, QA_W), lambda b, i: (b, i, 0))
    full = pl.BlockSpec((1, tm, D), lambda b, i: (b, i, 0))
    return pl.pallas_call(
        _out_kernel,
        grid=(B, S // tm),
        in_specs=[full, half, half, half, half, half, half, half,
                  pl.BlockSpec((1, QA_W), const), pl.BlockSpec((1, QB_W), const),
                  pl.BlockSpec((QA_W + QB_W, D), const, pipeline_mode=pl.Buffered(1))],
        out_specs=full,
        out_shape=jax.ShapeDtypeStruct((B, S, D), _F32),
        compiler_params=_cparams(2),
        name="out_proj",
    )(x, ya, *os_, *lses, ga, gb, w_out)


def _mlp_kernel(x_ref, n2_ref, w1_ref, w2_ref, fn_ref, out_ref, *, chunk, final_norm):
    x = x_ref[0]
    h = (x * _rms_scale(x) * n2_ref[...]).astype(_BF16)
    acc = x
    for c in range(D_FF // chunk):
        sl = slice(c * chunk, (c + 1) * chunk)
        u = jnp.maximum(_dot(h, w1_ref[:, sl]), 0.0)
        acc = acc + _dot((u * u).astype(_BF16), w2_ref[sl, :])
    if final_norm:
        acc = acc * _rms_scale(acc) * fn_ref[...]
    out_ref[0] = acc


def _mlp(x, n2, w1, w2, fn, tm, final_norm):
    B, S, D = x.shape
    const = lambda b, i: (0, 0)
    full = pl.BlockSpec((1, tm, D), lambda b, i: (b, i, 0))
    return pl.pallas_call(
        functools.partial(_mlp_kernel, chunk=1024, final_norm=final_norm),
        grid=(B, S // tm),
        in_specs=[full, pl.BlockSpec((1, D), const),
                  pl.BlockSpec((D, D_FF), const, pipeline_mode=pl.Buffered(1)),
                  pl.BlockSpec((D_FF, D), const, pipeline_mode=pl.Buffered(1)),
                  pl.BlockSpec((1, D), const)],
        out_specs=full,
        out_shape=jax.ShapeDtypeStruct((B, S, D), _F32),
        compiler_params=_cparams(2),
        name="mlp",
    )(x, n2, w1, w2, fn)


def kernel(x, norm1, w_in, q_norm, k_norm, out_norm_a, out_norm_b, w_out,
           norm2, w_mlp_in, w_mlp_out, final_norm):
    B, S, D = x.shape
    depth = w_in.shape[0]
    tables = _rope_tables(S)
    gm = _group_mean_matrix()
    row = lambda a: a.reshape(1, -1).astype(_F32)
    fn = row(final_norm)
    for l in range(depth):
        qg = row(jnp.tile(q_norm[l], LANES // HEAD_DIM))
        kg = row(jnp.tile(k_norm[l], LANES // HEAD_DIM))
        qa, ka, vt, qb, kb, vb = _in_proj(
            x, row(norm1[l]), w_in[l].astype(_BF16), qg, kg, gm, tables, tm=512)
        ya = _gqa(qa, ka, vt, tq=256)
        os_, lses = [], []
        for window, dil in DILATED_PATTERNS:
            o, lse = _dilated(qb, kb, vb, window, dil, tq=256)
            os_.append(o)
            lses.append(lse)
        x = _out_proj(x, ya, os_, lses, row(out_norm_a[l]), row(out_norm_b[l]),
                      w_out[l].astype(_BF16), tm=512)
        x = _mlp(x, row(norm2[l]), w_mlp_in[l].astype(_BF16), w_mlp_out[l].astype(_BF16),
                 fn, tm=512, final_norm=(l == depth - 1))
    return x
```

```python
import functools

import jax
import jax.numpy as jnp
import numpy as np
from jax import lax
from jax.experimental import pallas as pl
from jax.experimental.pallas import tpu as pltpu

D_MODEL = 1024
HEAD_DIM = 64
N_HEADS_A = 8
N_KV_A = 2
N_HEADS_B = 8
D_FF = 4 * D_MODEL
GRID_W = 64
AXIAL_THETA = 10000.0
ROPE_THETA = 500000.0
ROPE_DIM = HEAD_DIM // 4
DILATED_PATTERNS = ((128, 1), (512, 4), (2048, 16))
NORM_EPS = 1e-6
NEG_INF = -1e30

QA_W = N_HEADS_A * HEAD_DIM
KVA_W = N_KV_A * HEAD_DIM
QB_W = N_HEADS_B * HEAD_DIM
IN_W = QA_W + 2 * KVA_W + 3 * QB_W
SCALE = HEAD_DIM ** -0.5
SCALE_A = SCALE * float(np.log2(np.e))

LANES = 128
VMEM_LIMIT = 56 * 1024 * 1024

_F32 = jnp.float32
_BF16 = jnp.bfloat16


def _cparams(n_grid):
    return pltpu.CompilerParams(
        dimension_semantics=("arbitrary",) * n_grid, vmem_limit_bytes=VMEM_LIMIT)


def _dot(a, b):
    return jnp.dot(a, b, preferred_element_type=_F32)


def _dot_nt(a, b):
    return lax.dot_general(a, b, (((1,), (1,)), ((), ())), preferred_element_type=_F32)


def _rms_scale(x):
    return lax.rsqrt(jnp.mean(x * x, axis=-1, keepdims=True) + NORM_EPS)


def _rope_tables(seq):
    lane = np.arange(LANES)
    e = lane % HEAD_DIM
    t = jnp.arange(seq, dtype=jnp.int32)
    half = HEAD_DIM // 4
    freqs = AXIAL_THETA ** (-jnp.arange(half, dtype=_F32) / half)
    f_idx = e % half
    pos = jnp.where((e < HEAD_DIM // 2)[None, :], (t // GRID_W)[:, None], (t % GRID_W)[:, None])
    ang = pos.astype(_F32) * freqs[f_idx][None, :]
    sign = np.where((e % (2 * half)) < half, -1.0, 1.0).astype(np.float32)
    ax_cos = jnp.cos(ang)
    ax_sin = jnp.sin(ang) * sign[None, :]
    halfp = ROPE_DIM // 2
    freqs_p = ROPE_THETA ** (-jnp.arange(halfp, dtype=_F32) / halfp)
    angp = t.astype(_F32)[:, None] * freqs_p[e % halfp][None, :]
    rot = (e < ROPE_DIM)[None, :]
    signp = np.where(e < halfp, -1.0, 1.0).astype(np.float32)
    pr_cos = jnp.where(rot, jnp.cos(angp), 1.0)
    pr_sin = jnp.where(rot, jnp.sin(angp) * signp[None, :], 0.0)
    return ax_cos, ax_sin, pr_cos, pr_sin


def _group_mean_matrix():
    g = np.arange(LANES) // HEAD_DIM
    return jnp.asarray((g[:, None] == g[None, :]).astype(np.float32) / HEAD_DIM, dtype=_BF16)


def _in_kernel(x_ref, n1_ref, w_ref, qg_ref, kg_ref, gm_ref,
               axc_ref, axs_ref, prc_ref, prs_ref,
               qa_ref, ka_ref, vt_ref, qb_ref, kb_ref, vb_ref):
    x = x_ref[0]
    h = (x * _rms_scale(x) * n1_ref[...]).astype(_BF16)
    tm = x.shape[0]
    lane = lax.broadcasted_iota(jnp.int32, (tm, LANES), 1)
    e = lane % HEAD_DIM
    low = lane < HEAD_DIM
    gm = gm_ref[...]

    def head_norm(v, gain):
        sq = v * v
        hi = sq.astype(_BF16)
        lo = (sq - hi.astype(_F32)).astype(_BF16)
        ms = _dot(hi, gm) + _dot(lo, gm)
        return v * lax.rsqrt(ms + NORM_EPS) * gain

    def axial(v):
        first = (e % 32) < 16
        partner = jnp.where(first, pltpu.roll(v, LANES - 16, 1), pltpu.roll(v, 16, 1))
        return v * axc_ref[...] + partner * axs_ref[...]

    def partial(v):
        partner = jnp.where(e < 8, pltpu.roll(v, LANES - 8, 1), pltpu.roll(v, 8, 1))
        return v * prc_ref[...] + partner * prs_ref[...]

    pa = _dot(h, w_ref[:, 0:QA_W])
    for c in range(QA_W // LANES):
        v = axial(head_norm(pa[:, c * LANES:(c + 1) * LANES], qg_ref[...])) * SCALE_A
        swapped = pltpu.roll(v, HEAD_DIM, 1)
        for j in range(2):
            hd = 2 * c + j
            grp = hd // (N_HEADS_A // N_KV_A)
            src = v if j == grp else swapped
            keep = low if grp == 0 else jnp.logical_not(low)
            qa_ref[0, hd] = jnp.where(keep, src, 0.0).astype(_BF16)

    pkv = _dot(h, w_ref[:, QA_W:QA_W + 2 * KVA_W])
    ka_ref[0] = axial(head_norm(pkv[:, 0:LANES], kg_ref[...])).astype(_BF16)
    vt = pkv[:, LANES:2 * LANES].T
    ones = jnp.ones((HEAD_DIM, tm), _F32)
    for g in range(N_KV_A):
        vg = vt[g * HEAD_DIM:(g + 1) * HEAD_DIM]
        vt_ref[0, g, 0] = jnp.concatenate([vg, ones], axis=0).astype(_BF16)

    o = QA_W + 2 * KVA_W
    pq = _dot(h, w_ref[:, o:o + QB_W])
    pk = _dot(h, w_ref[:, o + QB_W:o + 2 * QB_W])
    for c in range(QB_W // LANES):
        sl = slice(c * LANES, (c + 1) * LANES)
        qb_ref[0, :, sl] = (partial(pq[:, sl]) * SCALE).astype(_BF16)
        kb_ref[0, :, sl] = partial(pk[:, sl]).astype(_BF16)
    vb_ref[0] = _dot(h, w_ref[:, o + 2 * QB_W:o + 3 * QB_W]).astype(_BF16)


def _in_proj(x, n1, w_in, qg, kg, gm, tables, tm):
    B, S, D = x.shape
    nt = S // tm
    const = lambda b, i: (0, 0)
    tab = pl.BlockSpec((tm, LANES), lambda b, i: (i, 0))
    out_shape = (
        jax.ShapeDtypeStruct((B, N_HEADS_A, S, LANES), _BF16),
        jax.ShapeDtypeStruct((B, S, LANES), _BF16),
        jax.ShapeDtypeStruct((B, N_KV_A, nt, LANES, tm), _BF16),
        jax.ShapeDtypeStruct((B, S, QB_W), _BF16),
        jax.ShapeDtypeStruct((B, S, QB_W), _BF16),
        jax.ShapeDtypeStruct((B, S, QB_W), _BF16),
    )
    wide = pl.BlockSpec((1, tm, QB_W), lambda b, i: (b, i, 0))
    narrow = pl.BlockSpec((1, tm, LANES), lambda b, i: (b, i, 0))
    return pl.pallas_call(
        _in_kernel,
        grid=(B, nt),
        in_specs=[
            pl.BlockSpec((1, tm, D), lambda b, i: (b, i, 0)),
            pl.BlockSpec((1, D), const),
            pl.BlockSpec((D, IN_W), const, pipeline_mode=pl.Buffered(1)),
            pl.BlockSpec((1, LANES), const),
            pl.BlockSpec((1, LANES), const),
            pl.BlockSpec((LANES, LANES), const),
            tab, tab, tab, tab,
        ],
        out_specs=(
            pl.BlockSpec((1, N_HEADS_A, tm, LANES), lambda b, i: (b, 0, i, 0)),
            narrow,
            pl.BlockSpec((1, N_KV_A, 1, LANES, tm), lambda b, i: (b, 0, i, 0, 0)),
            wide, wide, wide,
        ),
        out_shape=out_shape,
        compiler_params=_cparams(2),
        name="in_proj",
    )(x, n1, w_in, qg, kg, gm, *tables)


def _gqa_kernel(q_ref, k_ref, vt_ref, o_ref, acc_ref, st_ref):
    n_q, tq = q_ref.shape[1], q_ref.shape[2]
    cols = n_q * tq
    n_kv, tk = vt_ref.shape[2], vt_ref.shape[4]
    q = q_ref[0].reshape(cols, LANES)
    acc_ref[...] = jnp.zeros(acc_ref.shape, _F32)

    def scores(i):
        return _dot_nt(k_ref[0, i * tk:(i + 1) * tk, :], q)

    m_prev = jnp.full((1, cols), NEG_INF, _F32)
    st_ref[0] = scores(0)
    for i in range(n_kv):
        if i + 1 < n_kv:
            st_ref[(i + 1) % 2] = scores(i + 1)
        st = st_ref[i % 2]
        m_new = jnp.maximum(m_prev, jnp.max(st, axis=0, keepdims=True))
        alpha = jnp.exp2(m_prev - m_new)
        p = jnp.exp2(st - m_new).astype(_BF16)
        acc_ref[...] = alpha * acc_ref[...] + _dot(vt_ref[0, 0, i], p)
        m_prev = m_new
    acc = acc_ref[...]
    out_t = acc[0:HEAD_DIM] / acc[HEAD_DIM:2 * HEAD_DIM]
    y_t = jnp.concatenate([out_t[:, j * tq:(j + 1) * tq] for j in range(n_q)], axis=0)
    o_ref[0] = y_t.T


def _gqa(qa, ka, vt, tq):
    B, H, S, _ = qa.shape
    per = H // N_KV_A
    n_kv, tk = vt.shape[2], vt.shape[4]
    return pl.pallas_call(
        _gqa_kernel,
        grid=(B, N_KV_A, S // tq),
        in_specs=[
            pl.BlockSpec((1, per, tq, LANES), lambda b, g, i: (b, g, i, 0)),
            pl.BlockSpec((1, S, LANES), lambda b, g, i: (b, 0, 0)),
            pl.BlockSpec((1, 1, n_kv, LANES, tk), lambda b, g, i: (b, g, 0, 0, 0)),
        ],
        out_specs=pl.BlockSpec((1, tq, per * HEAD_DIM), lambda b, g, i: (b, i, g)),
        out_shape=jax.ShapeDtypeStruct((B, S, QA_W), _F32),
        scratch_shapes=[pltpu.VMEM((LANES, per * tq), _F32),
                        pltpu.VMEM((2, tk, per * tq), _F32)],
        compiler_params=_cparams(3),
        name="gqa_attn",
    )(qa, ka, vt)


class _Pattern:
    def __init__(self, window, dil, ni_q, unroll):
        self.dil = dil
        self.unroll = unroll
        self.n_side = window // (2 * dil)
        self.n_rows = RES // dil
        self.ni_q = ni_q
        halo = -(-self.n_side // self.n_rows)
        halo = -(-halo // BF16_ROWS) * BF16_ROWS
        self.ni_k = min(ni_q + 2 * halo, N_I)
        self.halo = halo
        self.n_blk = N_I // ni_q
        self.offs = sorted({min(max(b * ni_q - halo, 0), N_I - self.ni_k) * -1 + b * ni_q
                            for b in range(self.n_blk)})

    def bias(self):
        a_q, i_q = np.divmod(np.arange(self.n_rows * self.ni_q), self.ni_q)
        a_k, i_k = np.divmod(np.arange(self.n_rows * self.ni_k), self.ni_k)
        out = []
        for off in self.offs:
            dj = (self.n_rows * (off + i_q[:, None] - i_k[None, :])
                  + (a_q[:, None] - a_k[None, :]))
            out.append(np.where(np.abs(dj) <= self.n_side, 0.0, NEG_INF))
        return jnp.asarray(np.stack(out), dtype=_F32)


RES = 16
N_I = 4096 // RES
BF16_ROWS = 16
PATTERNS = tuple(_Pattern(w, d, n, u)
                 for (w, d), n, u in zip(DILATED_PATTERNS, (16, 32, N_I), (2, 8, 4)))


def _dil_kernel(q_ref, k_ref, v_ref, b0_ref, b1_ref, b2_ref, out_ref, o_s, l_s):
    bias_refs = (b0_ref, b1_ref, b2_ref)

    def attend(q, kw, vw, bias):
        nq, nk = q.shape[0], kw.shape[0]
        low = lax.broadcasted_iota(jnp.int32, (nq, LANES), 1) < HEAD_DIM
        vext = jnp.concatenate([vw, jnp.ones((nk, LANES), _BF16)], axis=1)
        outs, lses = [], []
        for keep in (low, jnp.logical_not(low)):
            qh = jnp.where(keep, q, jnp.zeros_like(q))
            s = _dot_nt(qh, kw) + bias
            m = jnp.max(s, axis=-1, keepdims=True)
            p = jnp.exp(s - m).astype(_BF16)
            pv = _dot(p, vext)
            l = pv[:, LANES:2 * LANES]
            outs.append(pv[:, 0:LANES] / l)
            lses.append(m + jnp.log(l))
        return jnp.where(low, outs[0], outs[1]), jnp.where(low, lses[0], lses[1])

    for pi, pat in enumerate(PATTERNS):
        n_var = len(pat.offs)

        def body(blk, carry, pi=pi, pat=pat, n_var=n_var):
            rho = blk // pat.n_blk
            ib = blk % pat.n_blk
            i0 = pl.multiple_of(ib * pat.ni_q, pat.ni_q)
            ws = pl.multiple_of(jnp.clip(i0 - pat.halo, 0, N_I - pat.ni_k), BF16_ROWS)
            var = jnp.where(ib == 0, 0, jnp.where(ib == pat.n_blk - 1, n_var - 1, 1)) if n_var > 1 else 0
            rows = [rho + pat.dil * a for a in range(pat.n_rows)]
            q = jnp.concatenate([q_ref[0, r, pl.ds(i0, pat.ni_q), :] for r in rows], axis=0)
            kw = jnp.concatenate([k_ref[0, r, pl.ds(ws, pat.ni_k), :] for r in rows], axis=0)
            vw = jnp.concatenate([v_ref[0, r, pl.ds(ws, pat.ni_k), :] for r in rows], axis=0)
            o, lse = attend(q, kw, vw, bias_refs[pi][var])
            for a, r in enumerate(rows):
                sl = slice(a * pat.ni_q, (a + 1) * pat.ni_q)
                o_s[pi, r, pl.ds(i0, pat.ni_q), :] = o[sl]
                l_s[pi, r, pl.ds(i0, pat.ni_q), :] = lse[sl]
            return carry

        lax.fori_loop(0, pat.dil * pat.n_blk, body, 0, unroll=pat.unroll)

    def merge(r, carry):
        l1, l2, l3 = l_s[0, r], l_s[1, r], l_s[2, r]
        mx = jnp.maximum(jnp.maximum(l1, l2), l3)
        e1, e2, e3 = jnp.exp(l1 - mx), jnp.exp(l2 - mx), jnp.exp(l3 - mx)
        out_ref[0, r] = (e1 * o_s[0, r] + e2 * o_s[1, r] + e3 * o_s[2, r]) / (e1 + e2 + e3)
        return carry

    lax.fori_loop(0, RES, merge, 0)


def _dilated(qb, kb, vb, biases):
    B, S, W = qb.shape
    view = lambda a: a.reshape(B, RES, N_I, W)
    spec = pl.BlockSpec((1, RES, N_I, LANES), lambda b, c: (b, 0, 0, c))
    bspecs = [pl.BlockSpec(bb.shape, lambda b, c: (0, 0, 0)) for bb in biases]
    yb = pl.pallas_call(
        _dil_kernel,
        grid=(B, W // LANES),
        in_specs=[spec, spec, spec] + bspecs,
        out_specs=spec,
        out_shape=jax.ShapeDtypeStruct((B, RES, N_I, W), _F32),
        scratch_shapes=[pltpu.VMEM((len(PATTERNS), RES, N_I, LANES), _F32)] * 2,
        compiler_params=_cparams(2),
        name="dilated",
    )(view(qb), view(kb), view(vb), *biases)
    return yb.reshape(B, S, W)


def _post_kernel(x_ref, ya_ref, yb_ref, ga_ref, gb_ref, wo_ref, n2_ref, w1_ref, w2_ref,
                 fn_ref, out_ref, *, chunk, final_norm):
    ya, yb = ya_ref[0], yb_ref[0]
    ya_n = (ya * _rms_scale(ya) * ga_ref[...]).astype(_BF16)
    yb_n = (yb * _rms_scale(yb) * gb_ref[...]).astype(_BF16)
    x = (x_ref[0] + _dot(ya_n, wo_ref[0:QA_W, :])
         + _dot(yb_n, wo_ref[QA_W:QA_W + QB_W, :]))
    h = (x * _rms_scale(x) * n2_ref[...]).astype(_BF16)
    acc = jnp.zeros_like(x)
    for c in range(D_FF // chunk):
        sl = slice(c * chunk, (c + 1) * chunk)
        u = jnp.maximum(_dot(h, w1_ref[:, sl]), 0.0)
        acc = acc + _dot((u * u).astype(_BF16), w2_ref[sl, :])
    acc = x + acc
    if final_norm:
        acc = acc * _rms_scale(acc) * fn_ref[...]
    out_ref[0] = acc


def _post(x, ya, yb, ga, gb, w_out, n2, w1, w2, fn, tm, final_norm):
    B, S, D = x.shape
    const = lambda b, i: (0, 0)
    half = pl.BlockSpec((1, tm, QA_W), lambda b, i: (b, i, 0))
    full = pl.BlockSpec((1, tm, D), lambda b, i: (b, i, 0))
    resident = lambda shape: pl.BlockSpec(shape, const, pipeline_mode=pl.Buffered(1))
    return pl.pallas_call(
        functools.partial(_post_kernel, chunk=1024, final_norm=final_norm),
        grid=(B, S // tm),
        in_specs=[full, half, half,
                  pl.BlockSpec((1, QA_W), const), pl.BlockSpec((1, QB_W), const),
                  resident((QA_W + QB_W, D)),
                  pl.BlockSpec((1, D), const), resident((D, D_FF)), resident((D_FF, D)),
                  pl.BlockSpec((1, D), const)],
        out_specs=full,
        out_shape=jax.ShapeDtypeStruct((B, S, D), _F32),
        compiler_params=_cparams(2),
        name="post",
    )(x, ya, yb, ga, gb, w_out, n2, w1, w2, fn)


def kernel(x, norm1, w_in, q_norm, k_norm, out_norm_a, out_norm_b, w_out,
           norm2, w_mlp_in, w_mlp_out, final_norm):
    B, S, D = x.shape
    assert S == RES * N_I
    depth = w_in.shape[0]
    to_res = lambda a: jnp.swapaxes(a.reshape(a.shape[:-2] + (N_I, RES, a.shape[-1])), -3, -2
                                    ).reshape(a.shape)
    from_res = lambda a: jnp.swapaxes(a.reshape(a.shape[:-2] + (RES, N_I, a.shape[-1])), -3, -2
                                      ).reshape(a.shape)
    tables = [to_res(t) for t in _rope_tables(S)]
    biases = [p.bias() for p in PATTERNS]
    gm = _group_mean_matrix()
    row = lambda a: a.reshape(1, -1).astype(_F32)
    fn = row(final_norm)
    x = to_res(x)
    for l in range(depth):
        qg = row(jnp.tile(q_norm[l], LANES // HEAD_DIM))
        kg = row(jnp.tile(k_norm[l], LANES // HEAD_DIM))
        qa, ka, vt, qb, kb, vb = _in_proj(
            x, row(norm1[l]), w_in[l].astype(_BF16), qg, kg, gm, tables, tm=512)
        ya = _gqa(qa, ka, vt, tq=256)
        yb = _dilated(qb, kb, vb, biases)
        x = _post(x, ya, yb, row(out_norm_a[l]), row(out_norm_b[l]), w_out[l].astype(_BF16),
                  row(norm2[l]), w_mlp_in[l].astype(_BF16), w_mlp_out[l].astype(_BF16),
                  fn, tm=512, final_norm=(l == depth - 1))
    return from_res(x)
```

```python
import functools

import jax
import jax.numpy as jnp
import numpy as np
from jax import lax
from jax.experimental import pallas as pl
from jax.experimental.pallas import tpu as pltpu

D_MODEL = 1024
HEAD_DIM = 64
N_HEADS_A = 8
N_KV_A = 2
N_HEADS_B = 8
D_FF = 4 * D_MODEL
GRID_W = 64
AXIAL_THETA = 10000.0
ROPE_THETA = 500000.0
ROPE_DIM = HEAD_DIM // 4
DILATED_PATTERNS = ((128, 1), (512, 4), (2048, 16))
NORM_EPS = 1e-6
NEG_INF = -1e30

QA_W = N_HEADS_A * HEAD_DIM
KVA_W = N_KV_A * HEAD_DIM
QB_W = N_HEADS_B * HEAD_DIM
IN_W = QA_W + 2 * KVA_W + 3 * QB_W
SCALE_LOG2 = HEAD_DIM ** -0.5 * float(np.log2(np.e))

LANES = 128
VMEM_LIMIT = 56 * 1024 * 1024

_F32 = jnp.float32
_BF16 = jnp.bfloat16


def _cparams(n_grid):
    return pltpu.CompilerParams(
        dimension_semantics=("arbitrary",) * n_grid, vmem_limit_bytes=VMEM_LIMIT)


def _dot(a, b):
    return jnp.dot(a, b, preferred_element_type=_F32)


def _dot_nt(a, b):
    return lax.dot_general(a, b, (((1,), (1,)), ((), ())), preferred_element_type=_F32)


def _rms_scale(x):
    return lax.rsqrt(jnp.mean(x * x, axis=-1, keepdims=True) + NORM_EPS)


def _rope_tables(seq):
    lane = np.arange(LANES)
    e = lane % HEAD_DIM
    t = jnp.arange(seq, dtype=jnp.int32)
    half = HEAD_DIM // 4
    freqs = AXIAL_THETA ** (-jnp.arange(half, dtype=_F32) / half)
    f_idx = e % half
    pos = jnp.where((e < HEAD_DIM // 2)[None, :], (t // GRID_W)[:, None], (t % GRID_W)[:, None])
    ang = pos.astype(_F32) * freqs[f_idx][None, :]
    sign = np.where((e % (2 * half)) < half, -1.0, 1.0).astype(np.float32)
    ax_cos = jnp.cos(ang)
    ax_sin = jnp.sin(ang) * sign[None, :]
    halfp = ROPE_DIM // 2
    freqs_p = ROPE_THETA ** (-jnp.arange(halfp, dtype=_F32) / halfp)
    angp = t.astype(_F32)[:, None] * freqs_p[e % halfp][None, :]
    rot = (e < ROPE_DIM)[None, :]
    signp = np.where(e < halfp, -1.0, 1.0).astype(np.float32)
    pr_cos = jnp.where(rot, jnp.cos(angp), 1.0)
    pr_sin = jnp.where(rot, jnp.sin(angp) * signp[None, :], 0.0)
    return ax_cos, ax_sin, pr_cos, pr_sin


def _group_mean_matrix():
    g = np.arange(LANES) // HEAD_DIM
    return jnp.asarray((g[:, None] == g[None, :]).astype(np.float32) / HEAD_DIM, dtype=_BF16)


def _in_kernel(x_ref, n1_ref, w_ref, qg_ref, kg_ref, gm_ref,
               axc_ref, axs_ref, prc_ref, prs_ref,
               qa_ref, ka_ref, vt_ref, qb_ref, kb_ref, vb_ref, ks_ref, vs_ref):
    x = x_ref[0]
    h = (x * _rms_scale(x) * n1_ref[...]).astype(_BF16)
    tm = x.shape[0]
    lane = lax.broadcasted_iota(jnp.int32, (tm, LANES), 1)
    e = lane % HEAD_DIM
    low = lane < HEAD_DIM
    gm = gm_ref[...]

    def head_norm(v, gain):
        sq = v * v
        hi = sq.astype(_BF16)
        lo = (sq - hi.astype(_F32)).astype(_BF16)
        ms = _dot(hi, gm) + _dot(lo, gm)
        return v * lax.rsqrt(ms + NORM_EPS) * gain

    def axial(v):
        first = (e % 32) < 16
        partner = jnp.where(first, pltpu.roll(v, LANES - 16, 1), pltpu.roll(v, 16, 1))
        return v * axc_ref[...] + partner * axs_ref[...]

    def partial(v):
        partner = jnp.where(e < 8, pltpu.roll(v, LANES - 8, 1), pltpu.roll(v, 8, 1))
        return v * prc_ref[...] + partner * prs_ref[...]

    pa = _dot(h, w_ref[:, 0:QA_W])
    for c in range(QA_W // LANES):
        v = axial(head_norm(pa[:, c * LANES:(c + 1) * LANES], qg_ref[...])) * SCALE_LOG2
        swapped = pltpu.roll(v, HEAD_DIM, 1)
        for j in range(2):
            hd = 2 * c + j
            grp = hd // (N_HEADS_A // N_KV_A)
            src = v if j == grp else swapped
            keep = low if grp == 0 else jnp.logical_not(low)
            qa_ref[0, hd] = jnp.where(keep, src, 0.0).astype(_BF16)

    pkv = _dot(h, w_ref[:, QA_W:QA_W + 2 * KVA_W])
    ka_ref[0] = axial(head_norm(pkv[:, 0:LANES], kg_ref[...])).astype(_BF16)
    vt = pkv[:, LANES:2 * LANES].T
    ones = jnp.ones((HEAD_DIM, tm), _F32)
    for g in range(N_KV_A):
        vg = vt[g * HEAD_DIM:(g + 1) * HEAD_DIM]
        vt_ref[0, g, 0] = jnp.concatenate([vg, ones], axis=0).astype(_BF16)

    def store_shifted(ref, sl, v):
        pad = jnp.zeros((KEY_SHIFT, v.shape[1]), _F32)
        for r in range(tm // N_I):
            rows = v[r * N_I:(r + 1) * N_I]
            ref[0, r, :, sl] = jnp.concatenate([pad, rows, pad], axis=0).astype(_BF16)

    o = QA_W + 2 * KVA_W
    pq = _dot(h, w_ref[:, o:o + QB_W])
    pk = _dot(h, w_ref[:, o + QB_W:o + 2 * QB_W])
    for c in range(QB_W // LANES):
        sl = slice(c * LANES, (c + 1) * LANES)
        qb_ref[0, :, sl] = (partial(pq[:, sl]) * SCALE_LOG2).astype(_BF16)
        kc = partial(pk[:, sl])
        kb_ref[0, :, sl] = kc.astype(_BF16)
        store_shifted(ks_ref, sl, kc)
    pv = _dot(h, w_ref[:, o + 2 * QB_W:o + 3 * QB_W])
    vb_ref[0] = pv.astype(_BF16)
    store_shifted(vs_ref, slice(0, QB_W), pv)


def _layer_weight(w, layer):
    return pl.BlockSpec((None,) + w.shape[1:], lambda b, i: (layer, 0, 0),
                        pipeline_mode=pl.Buffered(1))


def _in_proj(x, n1, w_in, layer, qg, kg, gm, tables, tm):
    B, S, D = x.shape
    nt = S // tm
    assert tm % N_I == 0
    const = lambda b, i: (0, 0)
    tab = pl.BlockSpec((tm, LANES), lambda b, i: (i, 0))
    out_shape = (
        jax.ShapeDtypeStruct((B, N_HEADS_A, S, LANES), _BF16),
        jax.ShapeDtypeStruct((B, S, LANES), _BF16),
        jax.ShapeDtypeStruct((B, N_KV_A, nt, LANES, tm), _BF16),
        jax.ShapeDtypeStruct((B, S, QB_W), _BF16),
        jax.ShapeDtypeStruct((B, S, QB_W), _BF16),
        jax.ShapeDtypeStruct((B, S, QB_W), _BF16),
        jax.ShapeDtypeStruct((B, RES, N_I_PAD, QB_W), _BF16),
        jax.ShapeDtypeStruct((B, RES, N_I_PAD, QB_W), _BF16),
    )
    shifted = pl.BlockSpec((1, tm // N_I, N_I_PAD, QB_W), lambda b, i: (b, i, 0, 0))
    wide = pl.BlockSpec((1, tm, QB_W), lambda b, i: (b, i, 0))
    narrow = pl.BlockSpec((1, tm, LANES), lambda b, i: (b, i, 0))
    return pl.pallas_call(
        _in_kernel,
        grid=(B, nt),
        in_specs=[
            pl.BlockSpec((1, tm, D), lambda b, i: (b, i, 0)),
            pl.BlockSpec((1, D), const),
            _layer_weight(w_in, layer),
            pl.BlockSpec((1, LANES), const),
            pl.BlockSpec((1, LANES), const),
            pl.BlockSpec((LANES, LANES), const),
            tab, tab, tab, tab,
        ],
        out_specs=(
            pl.BlockSpec((1, N_HEADS_A, tm, LANES), lambda b, i: (b, 0, i, 0)),
            narrow,
            pl.BlockSpec((1, N_KV_A, 1, LANES, tm), lambda b, i: (b, 0, i, 0, 0)),
            wide, wide, wide, shifted, shifted,
        ),
        out_shape=out_shape,
        compiler_params=_cparams(2),
        name="in_proj",
    )(x, n1, w_in, qg, kg, gm, *tables)


def _gqa_kernel(q_ref, k_ref, vt_ref, o_ref, acc_ref, st_ref):
    n_q, tq = q_ref.shape[1], q_ref.shape[2]
    cols = n_q * tq
    tk = st_ref.shape[1]
    per_tile = tk // vt_ref.shape[4]
    n_kv = vt_ref.shape[2] // per_tile
    q = q_ref[0].reshape(cols, LANES)
    acc_ref[...] = jnp.zeros(acc_ref.shape, _F32)

    def scores(i):
        return _dot_nt(k_ref[0, i * tk:(i + 1) * tk, :], q)

    m_prev = jnp.full((1, cols), NEG_INF, _F32)
    st_ref[0] = scores(0)
    for i in range(n_kv):
        if i + 1 < n_kv:
            st_ref[(i + 1) % 2] = scores(i + 1)
        st = st_ref[i % 2]
        m_new = jnp.maximum(m_prev, jnp.max(st, axis=0, keepdims=True))
        alpha = jnp.exp2(m_prev - m_new)
        p = jnp.exp2(st - m_new).astype(_BF16)
        vt = jnp.concatenate(
            [vt_ref[0, 0, i * per_tile + t] for t in range(per_tile)], axis=1)
        acc_ref[...] = alpha * acc_ref[...] + _dot(vt, p)
        m_prev = m_new
    acc = acc_ref[...]
    out_t = acc[0:HEAD_DIM] / acc[HEAD_DIM:2 * HEAD_DIM]
    y_t = jnp.concatenate([out_t[:, j * tq:(j + 1) * tq] for j in range(n_q)], axis=0)
    o_ref[0] = y_t.T.astype(o_ref.dtype)


def _gqa(qa, ka, vt, tq, tk):
    B, H, S, _ = qa.shape
    per = H // N_KV_A
    n_kv, tkv = vt.shape[2], vt.shape[4]
    assert tk % tkv == 0
    return pl.pallas_call(
        _gqa_kernel,
        grid=(B, N_KV_A, S // tq),
        in_specs=[
            pl.BlockSpec((1, per, tq, LANES), lambda b, g, i: (b, g, i, 0)),
            pl.BlockSpec((1, S, LANES), lambda b, g, i: (b, 0, 0)),
            pl.BlockSpec((1, 1, n_kv, LANES, tkv), lambda b, g, i: (b, g, 0, 0, 0)),
        ],
        out_specs=pl.BlockSpec((1, tq, per * HEAD_DIM), lambda b, g, i: (b, i, g)),
        out_shape=jax.ShapeDtypeStruct((B, S, QA_W), _BF16),
        scratch_shapes=[pltpu.VMEM((LANES, per * tq), _F32),
                        pltpu.VMEM((2, tk, per * tq), _F32)],
        compiler_params=_cparams(3),
        name="gqa_attn",
    )(qa, ka, vt)


class _Pattern:
    def __init__(self, window, dil, ni_q, unroll, shift=0):
        self.dil = dil
        self.unroll = unroll
        self.shift = shift
        self.n_side = window // (2 * dil)
        self.n_rows = RES // dil
        self.ni_q = ni_q
        need = -(-self.n_side // self.n_rows)
        self.halo = shift if shift else -(-need // BF16_ROWS) * BF16_ROWS
        assert self.halo >= need
        self.ni_k = min(ni_q + 2 * self.halo, N_I)
        self.n_blk = N_I // ni_q
        self.variants = sorted({0, min(1, self.n_blk - 1), self.n_blk - 1})

    def key_start(self, b):
        i0 = b * self.ni_q
        return i0 - self.shift if self.shift else min(max(i0 - self.halo, 0), N_I - self.ni_k)

    def _mask(self, b):
        a_q, i_q = np.divmod(np.arange(self.n_rows * self.ni_q), self.ni_q)
        a_k, i_k = np.divmod(np.arange(self.n_rows * self.ni_k), self.ni_k)
        i_q = i_q + b * self.ni_q
        i_k = i_k + self.key_start(b)
        dj = self.n_rows * (i_q[:, None] - i_k[None, :]) + (a_q[:, None] - a_k[None, :])
        ok = (np.abs(dj) <= self.n_side) & ((i_k >= 0) & (i_k < N_I))[None, :]
        return np.where(ok, 0.0, NEG_INF).astype(np.float32)

    def bias(self):
        for b in range(2, self.n_blk - 1):
            assert np.array_equal(self._mask(b), self._mask(1))
        return jnp.asarray(np.stack([self._mask(b) for b in self.variants]))


RES = 16
N_I = 4096 // RES
BF16_ROWS = 16
KEY_SHIFT = 8
N_I_PAD = N_I + 2 * KEY_SHIFT
PATTERNS = (_Pattern(128, 1, 16, 2, shift=KEY_SHIFT), _Pattern(512, 4, 32, 8),
            _Pattern(2048, 16, N_I, 4))
assert tuple((2 * p.n_side * p.dil, p.dil) for p in PATTERNS) == DILATED_PATTERNS


def _dil_kernel(q_ref, k_ref, v_ref, ks_ref, vs_ref, b0_ref, b1_ref, b2_ref, out_ref,
                a_s, m_s, l_s):
    bias_refs = (b0_ref, b1_ref, b2_ref)

    def attend(q, kw, vw, bias):
        nq, nk = q.shape[0], kw.shape[0]
        low = lax.broadcasted_iota(jnp.int32, (nq, LANES), 1) < HEAD_DIM
        vext = jnp.concatenate([vw, jnp.ones((nk, LANES), _BF16)], axis=1)
        acc, mx, den = [], [], []
        for keep in (low, jnp.logical_not(low)):
            qh = jnp.where(keep, q, jnp.zeros_like(q))
            s = _dot_nt(qh, kw) + bias
            m = jnp.max(s, axis=-1, keepdims=True)
            pv = _dot(jnp.exp2(s - m).astype(_BF16), vext)
            acc.append(pv[:, 0:LANES])
            den.append(pv[:, LANES:2 * LANES])
            mx.append(jnp.broadcast_to(m, (nq, LANES)))
        return tuple(jnp.where(low, t[0], t[1]) for t in (acc, mx, den))

    for pi, pat in enumerate(PATTERNS):
        n_var = len(pat.variants)
        kr, vr = (ks_ref, vs_ref) if pat.shift else (k_ref, v_ref)

        def body(blk, carry, pi=pi, pat=pat, n_var=n_var, kr=kr, vr=vr):
            rho = blk // pat.n_blk
            ib = blk % pat.n_blk
            i0 = pl.multiple_of(ib * pat.ni_q, pat.ni_q)
            if pat.shift:
                ws = i0
            else:
                ws = pl.multiple_of(jnp.clip(i0 - pat.halo, 0, N_I - pat.ni_k), BF16_ROWS)
            var = jnp.where(ib == 0, 0, jnp.where(ib == pat.n_blk - 1, n_var - 1, 1)) if n_var > 1 else 0
            rows = [rho + pat.dil * a for a in range(pat.n_rows)]
            q = jnp.concatenate([q_ref[0, r, pl.ds(i0, pat.ni_q), :] for r in rows], axis=0)
            kw = jnp.concatenate([kr[0, r, pl.ds(ws, pat.ni_k), :] for r in rows], axis=0)
            vw = jnp.concatenate([vr[0, r, pl.ds(ws, pat.ni_k), :] for r in rows], axis=0)
            acc, mx, den = attend(q, kw, vw, bias_refs[pi][var])
            for a, r in enumerate(rows):
                sl = slice(a * pat.ni_q, (a + 1) * pat.ni_q)
                a_s[pi, r, pl.ds(i0, pat.ni_q), :] = acc[sl]
                m_s[pi, r, pl.ds(i0, pat.ni_q), :] = mx[sl]
                l_s[pi, r, pl.ds(i0, pat.ni_q), :] = den[sl]
            return carry

        lax.fori_loop(0, pat.dil * pat.n_blk, body, 0, unroll=pat.unroll)

    def merge(r, carry):
        ms = [m_s[p, r] for p in range(len(PATTERNS))]
        top = functools.reduce(jnp.maximum, ms)
        es = [jnp.exp2(m - top) for m in ms]
        num = sum(e * a_s[p, r] for p, e in enumerate(es))
        den = sum(e * l_s[p, r] for p, e in enumerate(es))
        out_ref[0, r] = (num / den).astype(out_ref.dtype)
        return carry

    lax.fori_loop(0, RES, merge, 0)


def _dilated(qb, kb, vb, ks, vs, biases):
    B, S, W = qb.shape
    view = lambda a: a.reshape(B, RES, N_I, W)
    spec = pl.BlockSpec((1, RES, N_I, LANES), lambda b, c: (b, 0, 0, c))
    sspec = pl.BlockSpec((1, RES, N_I_PAD, LANES), lambda b, c: (b, 0, 0, c))
    bspecs = [pl.BlockSpec(bb.shape, lambda b, c: (0, 0, 0)) for bb in biases]
    yb = pl.pallas_call(
        _dil_kernel,
        grid=(B, W // LANES),
        in_specs=[spec, spec, spec, sspec, sspec] + bspecs,
        out_specs=spec,
        out_shape=jax.ShapeDtypeStruct((B, RES, N_I, W), _BF16),
        scratch_shapes=[pltpu.VMEM((len(PATTERNS), RES, N_I, LANES), _F32)] * 3,
        compiler_params=_cparams(2),
        name="dilated",
    )(view(qb), view(kb), view(vb), ks, vs, *biases)
    return yb.reshape(B, S, W)


def _post_kernel(x_ref, ya_ref, yb_ref, ga_ref, gb_ref, wo_ref, n2_ref, w1_ref, w2_ref,
                 fn_ref, out_ref, *, chunk, final_norm):
    ya, yb = ya_ref[0].astype(_F32), yb_ref[0].astype(_F32)
    ya_n = (ya * _rms_scale(ya) * ga_ref[...]).astype(_BF16)
    yb_n = (yb * _rms_scale(yb) * gb_ref[...]).astype(_BF16)
    x = (x_ref[0] + _dot(ya_n, wo_ref[0:QA_W, :])
         + _dot(yb_n, wo_ref[QA_W:QA_W + QB_W, :]))
    h = (x * _rms_scale(x) * n2_ref[...]).astype(_BF16)
    acc = jnp.zeros_like(x)
    for c in range(D_FF // chunk):
        sl = slice(c * chunk, (c + 1) * chunk)
        u = jnp.maximum(_dot(h, w1_ref[:, sl]), 0.0)
        acc = acc + _dot((u * u).astype(_BF16), w2_ref[sl, :])
    acc = x + acc
    if final_norm:
        acc = acc * _rms_scale(acc) * fn_ref[...]
    out_ref[0] = acc


def _post(x, ya, yb, ga, gb, w_out, n2, w1, w2, layer, fn, tm, final_norm):
    B, S, D = x.shape
    const = lambda b, i: (0, 0)
    half = pl.BlockSpec((1, tm, QA_W), lambda b, i: (b, i, 0))
    full = pl.BlockSpec((1, tm, D), lambda b, i: (b, i, 0))
    return pl.pallas_call(
        functools.partial(_post_kernel, chunk=1024, final_norm=final_norm),
        grid=(B, S // tm),
        in_specs=[full, half, half,
                  pl.BlockSpec((1, QA_W), const), pl.BlockSpec((1, QB_W), const),
                  _layer_weight(w_out, layer),
                  pl.BlockSpec((1, D), const), _layer_weight(w1, layer), _layer_weight(w2, layer),
                  pl.BlockSpec((1, D), const)],
        out_specs=full,
        out_shape=jax.ShapeDtypeStruct((B, S, D), _F32),
        compiler_params=_cparams(2),
        name="post",
    )(x, ya, yb, ga, gb, w_out, n2, w1, w2, fn)


def kernel(x, norm1, w_in, q_norm, k_norm, out_norm_a, out_norm_b, w_out,
           norm2, w_mlp_in, w_mlp_out, final_norm):
    B, S, D = x.shape
    assert S == RES * N_I
    depth = w_in.shape[0]
    to_res = lambda a: jnp.swapaxes(a.reshape(a.shape[:-2] + (N_I, RES, a.shape[-1])), -3, -2
                                    ).reshape(a.shape)
    from_res = lambda a: jnp.swapaxes(a.reshape(a.shape[:-2] + (RES, N_I, a.shape[-1])), -3, -2
                                      ).reshape(a.shape)
    tables = [to_res(t) for t in _rope_tables(S)]
    biases = [p.bias() for p in PATTERNS]
    gm = _group_mean_matrix()
    row = lambda a: a.reshape(1, -1).astype(_F32)
    fn = row(final_norm)
    w_in, w_out, w_mlp_in, w_mlp_out = (w.astype(_BF16) for w in (w_in, w_out, w_mlp_in, w_mlp_out))
    x = to_res(x)
    for l in range(depth):
        qg = row(jnp.tile(q_norm[l], LANES // HEAD_DIM))
        kg = row(jnp.tile(k_norm[l], LANES // HEAD_DIM))
        qa, ka, vt, qb, kb, vb, ks, vs = _in_proj(
            x, row(norm1[l]), w_in, l, qg, kg, gm, tables, tm=512)
        ya = _gqa(qa, ka, vt, tq=512, tk=512)
        yb = _dilated(qb, kb, vb, ks, vs, biases)
        x = _post(x, ya, yb, row(out_norm_a[l]), row(out_norm_b[l]), w_out,
                  row(norm2[l]), w_mlp_in, w_mlp_out, l,
                  fn, tm=512, final_norm=(l == depth - 1))
    return from_res(x)
```

```python
import functools

import jax
import jax.numpy as jnp
import numpy as np
from jax import lax
from jax.experimental import pallas as pl
from jax.experimental.pallas import tpu as pltpu

D_MODEL = 1024
HEAD_DIM = 64
N_HEADS_A = 8
N_KV_A = 2
N_HEADS_B = 8
D_FF = 4 * D_MODEL
GRID_W = 64
AXIAL_THETA = 10000.0
ROPE_THETA = 500000.0
ROPE_DIM = HEAD_DIM // 4
DILATED_PATTERNS = ((128, 1), (512, 4), (2048, 16))
NORM_EPS = 1e-6
NEG_INF = -1e30

QA_W = N_HEADS_A * HEAD_DIM
KVA_W = N_KV_A * HEAD_DIM
QB_W = N_HEADS_B * HEAD_DIM
IN_W = QA_W + 2 * KVA_W + 3 * QB_W
SCALE_LOG2 = HEAD_DIM ** -0.5 * float(np.log2(np.e))

LANES = 128
VMEM_LIMIT = 56 * 1024 * 1024

_F32 = jnp.float32
_BF16 = jnp.bfloat16


def _cparams(n_grid):
    return pltpu.CompilerParams(
        dimension_semantics=("arbitrary",) * n_grid, vmem_limit_bytes=VMEM_LIMIT)


def _dot(a, b):
    return jnp.dot(a, b, preferred_element_type=_F32)


def _dot_nt(a, b):
    return lax.dot_general(a, b, (((1,), (1,)), ((), ())), preferred_element_type=_F32)


def _rms_scale(x):
    return lax.rsqrt(jnp.mean(x * x, axis=-1, keepdims=True) + NORM_EPS)


def _rope_tables(seq):
    e = np.arange(LANES) % HEAD_DIM
    t = np.arange(seq, dtype=np.float64)
    half = HEAD_DIM // 4
    freqs = AXIAL_THETA ** (-np.arange(half, dtype=np.float64) / half)
    pos = np.where((e < HEAD_DIM // 2)[None, :], (t // GRID_W)[:, None], (t % GRID_W)[:, None])
    ang = pos * freqs[e % half][None, :]
    sign = np.where((e % (2 * half)) < half, -1.0, 1.0)
    ax_cos = np.cos(ang)
    ax_sin = np.sin(ang) * sign[None, :]
    halfp = ROPE_DIM // 2
    freqs_p = ROPE_THETA ** (-np.arange(halfp, dtype=np.float64) / halfp)
    angp = t[:, None] * freqs_p[e % halfp][None, :]
    rot = (e < ROPE_DIM)[None, :]
    signp = np.where(e < halfp, -1.0, 1.0)
    pr_cos = np.where(rot, np.cos(angp), 1.0)
    pr_sin = np.where(rot, np.sin(angp) * signp[None, :], 0.0)
    return tuple(a.astype(np.float32) for a in (ax_cos, ax_sin, pr_cos, pr_sin))


def _group_mean_matrix():
    g = np.arange(LANES) // HEAD_DIM
    return jnp.asarray((g[:, None] == g[None, :]).astype(np.float32) / HEAD_DIM, dtype=_BF16)


def _in_kernel(x_ref, n1_ref, w_ref, qg_ref, kg_ref, gm_ref,
               axc_ref, axs_ref, prc_ref, prs_ref,
               qa_ref, ka_ref, vt_ref, qb_ref, kb_ref, vb_ref,
               ks0_ref, vs0_ref, ks1_ref, vs1_ref):
    x = x_ref[0]
    h = (x * _rms_scale(x) * n1_ref[...]).astype(_BF16)
    tm = x.shape[0]
    lane = lax.broadcasted_iota(jnp.int32, (tm, LANES), 1)
    e = lane % HEAD_DIM
    low = lane < HEAD_DIM
    gm = gm_ref[...]

    def head_norm(v, gain):
        sq = v * v
        hi = sq.astype(_BF16)
        lo = (sq - hi.astype(_F32)).astype(_BF16)
        ms = _dot(hi, gm) + _dot(lo, gm)
        return v * lax.rsqrt(ms + NORM_EPS) * gain

    def axial(v):
        first = (e % 32) < 16
        partner = jnp.where(first, pltpu.roll(v, LANES - 16, 1), pltpu.roll(v, 16, 1))
        return v * axc_ref[...] + partner * axs_ref[...]

    def partial(v):
        partner = jnp.where(e < 8, pltpu.roll(v, LANES - 8, 1), pltpu.roll(v, 8, 1))
        return v * prc_ref[...] + partner * prs_ref[...]

    pa = _dot(h, w_ref[:, 0:QA_W])
    for c in range(QA_W // LANES):
        v = axial(head_norm(pa[:, c * LANES:(c + 1) * LANES], qg_ref[...])) * SCALE_LOG2
        swapped = pltpu.roll(v, HEAD_DIM, 1)
        for j in range(2):
            hd = 2 * c + j
            grp = hd // (N_HEADS_A // N_KV_A)
            src = v if j == grp else swapped
            keep = low if grp == 0 else jnp.logical_not(low)
            qa_ref[0, hd] = jnp.where(keep, src, 0.0).astype(_BF16)

    pkv = _dot(h, w_ref[:, QA_W:QA_W + 2 * KVA_W])
    ka_ref[0] = axial(head_norm(pkv[:, 0:LANES], kg_ref[...])).astype(_BF16)
    vt = pkv[:, LANES:2 * LANES].T
    ones = jnp.ones((HEAD_DIM, tm), _F32)
    for g in range(N_KV_A):
        vg = vt[g * HEAD_DIM:(g + 1) * HEAD_DIM]
        vt_ref[0, g, 0] = jnp.concatenate([vg, ones], axis=0).astype(_BF16)

    def store_shifted(refs, sl, v):
        half = ROW_PAD // 2
        pad = jnp.zeros((half, v.shape[1]), _F32)
        for r in range(tm // N_I):
            centred = jnp.concatenate([pad, v[r * N_I:(r + 1) * N_I], pad], axis=0)
            for ref, shift in zip(refs, KEY_SHIFTS):
                ref[0, r, :, sl] = pltpu.roll(
                    centred, (shift - half) % N_I_PAD, 0).astype(_BF16)

    o = QA_W + 2 * KVA_W
    pq = _dot(h, w_ref[:, o:o + QB_W])
    pk = _dot(h, w_ref[:, o + QB_W:o + 2 * QB_W])
    for c in range(QB_W // LANES):
        sl = slice(c * LANES, (c + 1) * LANES)
        qb_ref[0, :, sl] = (partial(pq[:, sl]) * SCALE_LOG2).astype(_BF16)
        kc = partial(pk[:, sl])
        kb_ref[0, :, sl] = kc.astype(_BF16)
        store_shifted((ks0_ref, ks1_ref), sl, kc)
    pv = _dot(h, w_ref[:, o + 2 * QB_W:o + 3 * QB_W])
    vb_ref[0] = pv.astype(_BF16)
    for c in range(QB_W // LANES):
        sl = slice(c * LANES, (c + 1) * LANES)
        store_shifted((vs0_ref, vs1_ref), sl, pv[:, sl])


def _layer_weight(w, layer):
    return pl.BlockSpec((None,) + w.shape[1:], lambda b, i: (layer, 0, 0),
                        pipeline_mode=pl.Buffered(1))


def _in_proj(x, n1, w_in, layer, qg, kg, gm, tables, tm):
    B, S, D = x.shape
    nt = S // tm
    assert tm % N_I == 0
    const = lambda b, i: (0, 0)
    tab = pl.BlockSpec((tm, LANES), lambda b, i: (i, 0))
    out_shape = (
        jax.ShapeDtypeStruct((B, N_HEADS_A, S, LANES), _BF16),
        jax.ShapeDtypeStruct((B, S, LANES), _BF16),
        jax.ShapeDtypeStruct((B, N_KV_A, nt, LANES, tm), _BF16),
        jax.ShapeDtypeStruct((B, S, QB_W), _BF16),
        jax.ShapeDtypeStruct((B, S, QB_W), _BF16),
        jax.ShapeDtypeStruct((B, S, QB_W), _BF16),
    ) + (jax.ShapeDtypeStruct((B, RES, N_I_PAD, QB_W), _BF16),) * (2 * len(KEY_SHIFTS))
    shifted = pl.BlockSpec((1, tm // N_I, N_I_PAD, QB_W), lambda b, i: (b, i, 0, 0))
    wide = pl.BlockSpec((1, tm, QB_W), lambda b, i: (b, i, 0))
    narrow = pl.BlockSpec((1, tm, LANES), lambda b, i: (b, i, 0))
    return pl.pallas_call(
        _in_kernel,
        grid=(B, nt),
        in_specs=[
            pl.BlockSpec((1, tm, D), lambda b, i: (b, i, 0)),
            pl.BlockSpec((1, D), const),
            _layer_weight(w_in, layer),
            pl.BlockSpec((1, LANES), const),
            pl.BlockSpec((1, LANES), const),
            pl.BlockSpec((LANES, LANES), const),
            tab, tab, tab, tab,
        ],
        out_specs=(
            pl.BlockSpec((1, N_HEADS_A, tm, LANES), lambda b, i: (b, 0, i, 0)),
            narrow,
            pl.BlockSpec((1, N_KV_A, 1, LANES, tm), lambda b, i: (b, 0, i, 0, 0)),
            wide, wide, wide,
        ) + (shifted,) * (2 * len(KEY_SHIFTS)),
        out_shape=out_shape,
        compiler_params=_cparams(2),
        name="in_proj",
    )(x, n1, w_in, qg, kg, gm, *tables)


def _gqa_kernel(q_ref, k_ref, vt_ref, o_ref, acc_ref, st_ref):
    n_q, tq = q_ref.shape[1], q_ref.shape[2]
    cols = n_q * tq
    tk = st_ref.shape[1]
    per_tile = tk // vt_ref.shape[4]
    n_kv = vt_ref.shape[2] // per_tile
    q = q_ref[0].reshape(cols, LANES)
    acc_ref[...] = jnp.zeros(acc_ref.shape, _F32)

    def scores(i):
        return _dot_nt(k_ref[0, i * tk:(i + 1) * tk, :], q)

    m_prev = jnp.full((1, cols), NEG_INF, _F32)
    st_ref[0] = scores(0)
    for i in range(n_kv):
        if i + 1 < n_kv:
            st_ref[(i + 1) % 2] = scores(i + 1)
        st = st_ref[i % 2]
        m_new = jnp.maximum(m_prev, jnp.max(st, axis=0, keepdims=True))
        alpha = jnp.exp2(m_prev - m_new)
        p = jnp.exp2(st - m_new).astype(_BF16)
        vt = jnp.concatenate(
            [vt_ref[0, 0, i * per_tile + t] for t in range(per_tile)], axis=1)
        acc_ref[...] = alpha * acc_ref[...] + _dot(vt, p)
        m_prev = m_new
    acc = acc_ref[...]
    out_t = acc[0:HEAD_DIM] / acc[HEAD_DIM:2 * HEAD_DIM]
    y_t = jnp.concatenate([out_t[:, j * tq:(j + 1) * tq] for j in range(n_q)], axis=0)
    o_ref[0] = y_t.T.astype(o_ref.dtype)


def _gqa(qa, ka, vt, tq, tk):
    B, H, S, _ = qa.shape
    per = H // N_KV_A
    n_kv, tkv = vt.shape[2], vt.shape[4]
    assert tk % tkv == 0
    return pl.pallas_call(
        _gqa_kernel,
        grid=(B, N_KV_A, S // tq),
        in_specs=[
            pl.BlockSpec((1, per, tq, LANES), lambda b, g, i: (b, g, i, 0)),
            pl.BlockSpec((1, S, LANES), lambda b, g, i: (b, 0, 0)),
            pl.BlockSpec((1, 1, n_kv, LANES, tkv), lambda b, g, i: (b, g, 0, 0, 0)),
        ],
        out_specs=pl.BlockSpec((1, tq, per * HEAD_DIM), lambda b, g, i: (b, i, g)),
        out_shape=jax.ShapeDtypeStruct((B, S, QA_W), _BF16),
        scratch_shapes=[pltpu.VMEM((LANES, per * tq), _F32),
                        pltpu.VMEM((2, tk, per * tq), _F32)],
        compiler_params=_cparams(3),
        name="gqa_attn",
    )(qa, ka, vt)


class _Pattern:
    def __init__(self, window, dil, ni_q, unroll, shifts=()):
        self.dil = dil
        self.unroll = unroll
        self.shifts = shifts
        self.n_sub = max(len(shifts), 1)
        self.n_side = window // (2 * dil)
        self.n_rows = RES // dil
        self.ni_q = ni_q
        need = -(-self.n_side // self.n_rows)
        if shifts:
            self.halo = need
            assert ni_q * self.n_sub == BF16_ROWS and (ni_q + 2 * need) % BF16_ROWS == 0
            assert all((h * ni_q - need + s) % BF16_ROWS == 0 and need <= s <= ROW_PAD - need
                       for h, s in enumerate(shifts))
        else:
            self.halo = -(-need // BF16_ROWS) * BF16_ROWS
        self.ni_k = min(ni_q + 2 * self.halo, N_I)
        self.n_blk = N_I // ni_q
        self.variants = sorted({0, min(1, self.n_blk - 1), self.n_blk - 1})

    def key_start(self, b):
        i0 = b * self.ni_q
        return i0 - self.halo if self.shifts else min(max(i0 - self.halo, 0), N_I - self.ni_k)

    def _mask(self, b):
        a_q, i_q = np.divmod(np.arange(self.n_rows * self.ni_q), self.ni_q)
        a_k, i_k = np.divmod(np.arange(self.n_rows * self.ni_k), self.ni_k)
        i_q = i_q + b * self.ni_q
        i_k = i_k + self.key_start(b)
        dj = self.n_rows * (i_q[:, None] - i_k[None, :]) + (a_q[:, None] - a_k[None, :])
        ok = (np.abs(dj) <= self.n_side) & ((i_k >= 0) & (i_k < N_I))[None, :]
        return np.where(ok, 0.0, NEG_INF).astype(np.float32)

    def bias(self):
        for b in range(2, self.n_blk - 1):
            assert np.array_equal(self._mask(b), self._mask(1))
        return jnp.asarray(np.stack([self._mask(b) for b in self.variants]))


RES = 16
N_I = 4096 // RES
BF16_ROWS = 16
ROW_PAD = BF16_ROWS
N_I_PAD = N_I + ROW_PAD
KEY_SHIFTS = (4, 12)
PATTERNS = (_Pattern(128, 1, 8, 4, shifts=KEY_SHIFTS), _Pattern(512, 4, 32, 8),
            _Pattern(2048, 16, N_I, 4))
assert tuple((2 * p.n_side * p.dil, p.dil) for p in PATTERNS) == DILATED_PATTERNS


def _dil_kernel(q_ref, k_ref, v_ref, ks0_ref, vs0_ref, ks1_ref, vs1_ref,
                b0_ref, b1_ref, b2_ref, out_ref, a_s, m_s, l_s):
    bias_refs = (b0_ref, b1_ref, b2_ref)
    shifted_refs = ((ks0_ref, vs0_ref), (ks1_ref, vs1_ref))

    def attend(q, kw, vw, bias):
        nq, nk = q.shape[0], kw.shape[0]
        low = lax.broadcasted_iota(jnp.int32, (nq, LANES), 1) < HEAD_DIM
        vext = jnp.concatenate([vw, jnp.ones((nk, LANES), _BF16)], axis=1)
        zero = jnp.zeros_like(q)
        q2 = jnp.concatenate([jnp.where(low, q, zero), jnp.where(low, zero, q)], axis=0)
        s = _dot_nt(q2, kw) + jnp.concatenate([bias, bias], axis=0)
        m = jnp.max(s, axis=-1, keepdims=True)
        pv = _dot(jnp.exp2(s - m).astype(_BF16), vext)
        mb = jnp.broadcast_to(m, (2 * nq, LANES))
        return tuple(jnp.where(low, t[0:nq], t[nq:])
                     for t in (pv[:, 0:LANES], mb, pv[:, LANES:2 * LANES]))

    for pi, pat in enumerate(PATTERNS):
        n_var = len(pat.variants)
        ni_load = pat.ni_q * pat.n_sub
        n_load = N_I // ni_load

        def body(blk, carry, pi=pi, pat=pat, n_var=n_var, ni_load=ni_load, n_load=n_load):
            rho = blk // n_load
            il = blk % n_load
            i0 = pl.multiple_of(il * ni_load, ni_load)
            rows = [rho + pat.dil * a for a in range(pat.n_rows)]
            qs = [q_ref[0, r, pl.ds(i0, ni_load), :] for r in rows]
            if pat.shifts:
                qs = [c.astype(_F32) for c in qs]
            for h in range(pat.n_sub):
                ib = il * pat.n_sub + h
                iq = pl.multiple_of(i0 + h * pat.ni_q, pat.ni_q)
                if pat.shifts:
                    kr, vr = shifted_refs[h]
                    ws = i0 + (h * pat.ni_q - pat.halo + pat.shifts[h])
                    q = jnp.concatenate(
                        [c[h * pat.ni_q:(h + 1) * pat.ni_q] for c in qs], axis=0).astype(_BF16)
                else:
                    kr, vr = k_ref, v_ref
                    ws = jnp.clip(i0 - pat.halo, 0, N_I - pat.ni_k)
                    q = jnp.concatenate(qs, axis=0)
                ws = pl.multiple_of(ws, BF16_ROWS)
                var = jnp.where(ib == 0, 0, jnp.where(ib == pat.n_blk - 1, n_var - 1, 1)) if n_var > 1 else 0
                kw = jnp.concatenate([kr[0, r, pl.ds(ws, pat.ni_k), :] for r in rows], axis=0)
                vw = jnp.concatenate([vr[0, r, pl.ds(ws, pat.ni_k), :] for r in rows], axis=0)
                acc, mx, den = attend(q, kw, vw, bias_refs[pi][var])
                for a, r in enumerate(rows):
                    sl = slice(a * pat.ni_q, (a + 1) * pat.ni_q)
                    a_s[pi, r, pl.ds(iq, pat.ni_q), :] = acc[sl]
                    m_s[pi, r, pl.ds(iq, pat.ni_q), :] = mx[sl]
                    l_s[pi, r, pl.ds(iq, pat.ni_q), :] = den[sl]
            return carry

        lax.fori_loop(0, pat.dil * n_load, body, 0, unroll=pat.unroll)

    def merge(r, carry):
        ms = [m_s[p, r] for p in range(len(PATTERNS))]
        top = functools.reduce(jnp.maximum, ms)
        es = [jnp.exp2(m - top) for m in ms]
        num = sum(e * a_s[p, r] for p, e in enumerate(es))
        den = sum(e * l_s[p, r] for p, e in enumerate(es))
        out_ref[0, r] = (num / den).astype(out_ref.dtype)
        return carry

    lax.fori_loop(0, RES, merge, 0)


def _dilated(qb, kb, vb, shifted, biases):
    B, S, W = qb.shape
    view = lambda a: a.reshape(B, RES, N_I, W)
    spec = pl.BlockSpec((1, RES, N_I, LANES), lambda b, c: (b, 0, 0, c))
    sspec = pl.BlockSpec((1, RES, N_I_PAD, LANES), lambda b, c: (b, 0, 0, c))
    bspecs = [pl.BlockSpec(bb.shape, lambda b, c: (0, 0, 0)) for bb in biases]
    yb = pl.pallas_call(
        _dil_kernel,
        grid=(B, W // LANES),
        in_specs=[spec, spec, spec] + [sspec] * len(shifted) + bspecs,
        out_specs=spec,
        out_shape=jax.ShapeDtypeStruct((B, RES, N_I, W), _BF16),
        scratch_shapes=[pltpu.VMEM((len(PATTERNS), RES, N_I, LANES), _F32)] * 3,
        compiler_params=_cparams(2),
        name="dilated",
    )(view(qb), view(kb), view(vb), *shifted, *biases)
    return yb.reshape(B, S, W)


def _post_kernel(x_ref, ya_ref, yb_ref, ga_ref, gb_ref, wo_ref, n2_ref, w1_ref, w2_ref,
                 fn_ref, out_ref, *, chunk, final_norm):
    ya, yb = ya_ref[0].astype(_F32), yb_ref[0].astype(_F32)
    ya_n = (ya * _rms_scale(ya) * ga_ref[...]).astype(_BF16)
    yb_n = (yb * _rms_scale(yb) * gb_ref[...]).astype(_BF16)
    x = (x_ref[0] + _dot(ya_n, wo_ref[0:QA_W, :])
         + _dot(yb_n, wo_ref[QA_W:QA_W + QB_W, :]))
    h = (x * _rms_scale(x) * n2_ref[...]).astype(_BF16)
    acc = jnp.zeros_like(x)
    for c in range(D_FF // chunk):
        sl = slice(c * chunk, (c + 1) * chunk)
        u = jnp.maximum(_dot(h, w1_ref[:, sl]), 0.0)
        acc = acc + _dot((u * u).astype(_BF16), w2_ref[sl, :])
    acc = x + acc
    if final_norm:
        acc = acc * _rms_scale(acc) * fn_ref[...]
    out_ref[0] = acc


def _post(x, ya, yb, ga, gb, w_out, n2, w1, w2, layer, fn, tm, final_norm):
    B, S, D = x.shape
    const = lambda b, i: (0, 0)
    half = pl.BlockSpec((1, tm, QA_W), lambda b, i: (b, i, 0))
    full = pl.BlockSpec((1, tm, D), lambda b, i: (b, i, 0))
    return pl.pallas_call(
        functools.partial(_post_kernel, chunk=1024, final_norm=final_norm),
        grid=(B, S // tm),
        in_specs=[full, half, half,
                  pl.BlockSpec((1, QA_W), const), pl.BlockSpec((1, QB_W), const),
                  _layer_weight(w_out, layer),
                  pl.BlockSpec((1, D), const), _layer_weight(w1, layer), _layer_weight(w2, layer),
                  pl.BlockSpec((1, D), const)],
        out_specs=full,
        out_shape=jax.ShapeDtypeStruct((B, S, D), _F32),
        compiler_params=_cparams(2),
        name="post",
    )(x, ya, yb, ga, gb, w_out, n2, w1, w2, fn)


def kernel(x, norm1, w_in, q_norm, k_norm, out_norm_a, out_norm_b, w_out,
           norm2, w_mlp_in, w_mlp_out, final_norm):
    B, S, D = x.shape
    assert S == RES * N_I
    depth = w_in.shape[0]
    to_res = lambda a: jnp.swapaxes(a.reshape(a.shape[:-2] + (N_I, RES, a.shape[-1])), -3, -2
                                    ).reshape(a.shape)
    from_res = lambda a: jnp.swapaxes(a.reshape(a.shape[:-2] + (RES, N_I, a.shape[-1])), -3, -2
                                      ).reshape(a.shape)
    tables = [jnp.asarray(t.reshape(N_I, RES, LANES).swapaxes(0, 1).reshape(S, LANES))
              for t in _rope_tables(S)]
    biases = [p.bias() for p in PATTERNS]
    gm = _group_mean_matrix()
    row = lambda a: a.reshape(1, -1).astype(_F32)
    fn = row(final_norm)
    w_in, w_out, w_mlp_in, w_mlp_out = (w.astype(_BF16) for w in (w_in, w_out, w_mlp_in, w_mlp_out))
    x = to_res(x)
    for l in range(depth):
        qg = row(jnp.tile(q_norm[l], LANES // HEAD_DIM))
        kg = row(jnp.tile(k_norm[l], LANES // HEAD_DIM))
        qa, ka, vt, qb, kb, vb, *shifted = _in_proj(
            x, row(norm1[l]), w_in, l, qg, kg, gm, tables, tm=512)
        ya = _gqa(qa, ka, vt, tq=512, tk=512)
        yb = _dilated(qb, kb, vb, shifted, biases)
        x = _post(x, ya, yb, row(out_norm_a[l]), row(out_norm_b[l]), w_out,
                  row(norm2[l]), w_mlp_in, w_mlp_out, l,
                  fn, tm=512, final_norm=(l == depth - 1))
    return from_res(x)
```

```python
import functools

import jax
import jax.numpy as jnp
import numpy as np
from jax import lax
from jax.experimental import pallas as pl
from jax.experimental.pallas import tpu as pltpu

D_MODEL = 1024
HEAD_DIM = 64
N_HEADS_A = 8
N_KV_A = 2
N_HEADS_B = 8
D_FF = 4 * D_MODEL
GRID_W = 64
AXIAL_THETA = 10000.0
ROPE_THETA = 500000.0
ROPE_DIM = HEAD_DIM // 4
DILATED_PATTERNS = ((128, 1), (512, 4), (2048, 16))
NORM_EPS = 1e-6
NEG_INF = -1e30

QA_W = N_HEADS_A * HEAD_DIM
KVA_W = N_KV_A * HEAD_DIM
QB_W = N_HEADS_B * HEAD_DIM
IN_W = QA_W + 2 * KVA_W + 3 * QB_W
SCALE_LOG2 = HEAD_DIM ** -0.5 * float(np.log2(np.e))

LANES = 128
VMEM_LIMIT = 56 * 1024 * 1024

_F32 = jnp.float32
_BF16 = jnp.bfloat16


def _cparams(n_grid):
    return pltpu.CompilerParams(
        dimension_semantics=("arbitrary",) * n_grid, vmem_limit_bytes=VMEM_LIMIT)


def _dot(a, b):
    return jnp.dot(a, b, preferred_element_type=_F32)


def _dot_nt(a, b):
    return lax.dot_general(a, b, (((1,), (1,)), ((), ())), preferred_element_type=_F32)


def _rms_scale(x):
    return lax.rsqrt(jnp.mean(x * x, axis=-1, keepdims=True) + NORM_EPS)


def _rope_tables(seq):
    e = np.arange(LANES) % HEAD_DIM
    t = np.arange(seq, dtype=np.float64)
    half = HEAD_DIM // 4
    freqs = AXIAL_THETA ** (-np.arange(half, dtype=np.float64) / half)
    pos = np.where((e < HEAD_DIM // 2)[None, :], (t // GRID_W)[:, None], (t % GRID_W)[:, None])
    ang = pos * freqs[e % half][None, :]
    sign = np.where((e % (2 * half)) < half, -1.0, 1.0)
    ax_cos = np.cos(ang)
    ax_sin = np.sin(ang) * sign[None, :]
    halfp = ROPE_DIM // 2
    freqs_p = ROPE_THETA ** (-np.arange(halfp, dtype=np.float64) / halfp)
    angp = t[:, None] * freqs_p[e % halfp][None, :]
    rot = (e < ROPE_DIM)[None, :]
    signp = np.where(e < halfp, -1.0, 1.0)
    pr_cos = np.where(rot, np.cos(angp), 1.0)
    pr_sin = np.where(rot, np.sin(angp) * signp[None, :], 0.0)
    return tuple(a.astype(np.float32) for a in (ax_cos, ax_sin, pr_cos, pr_sin))


def _group_mean_matrix():
    g = np.arange(LANES) // HEAD_DIM
    return jnp.asarray((g[:, None] == g[None, :]).astype(np.float32) / HEAD_DIM, dtype=_BF16)


def _in_kernel(x_ref, n1_ref, w_ref, qg_ref, kg_ref, gm_ref,
               axc_ref, axs_ref, prc_ref, prs_ref,
               qa_ref, ka_ref, vt_ref, qb_ref, kb_ref, vb_ref,
               ks0_ref, vs0_ref, ks1_ref, vs1_ref):
    x = x_ref[0]
    h = (x * _rms_scale(x) * n1_ref[...]).astype(_BF16)
    tm = x.shape[0]
    lane = lax.broadcasted_iota(jnp.int32, (tm, LANES), 1)
    e = lane % HEAD_DIM
    low = lane < HEAD_DIM
    gm = gm_ref[...]

    def head_norm(v, gain):
        sq = v * v
        hi = sq.astype(_BF16)
        lo = (sq - hi.astype(_F32)).astype(_BF16)
        ms = _dot(hi, gm) + _dot(lo, gm)
        return v * lax.rsqrt(ms + NORM_EPS) * gain

    def axial(v):
        first = (e % 32) < 16
        partner = jnp.where(first, pltpu.roll(v, LANES - 16, 1), pltpu.roll(v, 16, 1))
        return v * axc_ref[...] + partner * axs_ref[...]

    def partial(v):
        partner = jnp.where(e < 8, pltpu.roll(v, LANES - 8, 1), pltpu.roll(v, 8, 1))
        return v * prc_ref[...] + partner * prs_ref[...]

    pa = _dot(h, w_ref[:, 0:QA_W])
    for c in range(QA_W // LANES):
        v = axial(head_norm(pa[:, c * LANES:(c + 1) * LANES], qg_ref[...])) * SCALE_LOG2
        swapped = pltpu.roll(v, HEAD_DIM, 1)
        for j in range(2):
            hd = 2 * c + j
            grp = hd // (N_HEADS_A // N_KV_A)
            src = v if j == grp else swapped
            keep = low if grp == 0 else jnp.logical_not(low)
            qa_ref[0, hd] = jnp.where(keep, src, 0.0).astype(_BF16)

    pkv = _dot(h, w_ref[:, QA_W:QA_W + 2 * KVA_W])
    ka_ref[0] = axial(head_norm(pkv[:, 0:LANES], kg_ref[...])).astype(_BF16)
    vt = pkv[:, LANES:2 * LANES].T
    ones = jnp.ones((HEAD_DIM, tm), _F32)
    for g in range(N_KV_A):
        vg = vt[g * HEAD_DIM:(g + 1) * HEAD_DIM]
        vt_ref[0, g, 0] = jnp.concatenate([vg, ones], axis=0).astype(_BF16)

    def store_shifted(refs, sl, v):
        half = ROW_PAD // 2
        pad = jnp.zeros((half, v.shape[1]), _F32)
        for r in range(tm // N_I):
            centred = jnp.concatenate([pad, v[r * N_I:(r + 1) * N_I], pad], axis=0)
            for ref, shift in zip(refs, KEY_SHIFTS):
                ref[0, r, :, sl] = pltpu.roll(
                    centred, (shift - half) % N_I_PAD, 0).astype(_BF16)

    o = QA_W + 2 * KVA_W
    pq = _dot(h, w_ref[:, o:o + QB_W])
    pk = _dot(h, w_ref[:, o + QB_W:o + 2 * QB_W])
    for c in range(QB_W // LANES):
        sl = slice(c * LANES, (c + 1) * LANES)
        qb_ref[0, :, sl] = (partial(pq[:, sl]) * SCALE_LOG2).astype(_BF16)
        kc = partial(pk[:, sl])
        kb_ref[0, :, sl] = kc.astype(_BF16)
        store_shifted((ks0_ref, ks1_ref), sl, kc)
    pv = _dot(h, w_ref[:, o + 2 * QB_W:o + 3 * QB_W])
    vb_ref[0] = pv.astype(_BF16)
    for c in range(QB_W // LANES):
        sl = slice(c * LANES, (c + 1) * LANES)
        store_shifted((vs0_ref, vs1_ref), sl, pv[:, sl])


def _layer_weight(w, layer):
    return pl.BlockSpec((None,) + w.shape[1:], lambda b, i: (layer, 0, 0),
                        pipeline_mode=pl.Buffered(1))


def _in_proj(x, n1, w_in, layer, qg, kg, gm, tables, tm):
    B, S, D = x.shape
    nt = S // tm
    assert tm % N_I == 0
    const = lambda b, i: (0, 0)
    tab = pl.BlockSpec((tm, LANES), lambda b, i: (i, 0))
    out_shape = (
        jax.ShapeDtypeStruct((B, N_HEADS_A, S, LANES), _BF16),
        jax.ShapeDtypeStruct((B, S, LANES), _BF16),
        jax.ShapeDtypeStruct((B, N_KV_A, nt, LANES, tm), _BF16),
        jax.ShapeDtypeStruct((B, S, QB_W), _BF16),
        jax.ShapeDtypeStruct((B, S, QB_W), _BF16),
        jax.ShapeDtypeStruct((B, S, QB_W), _BF16),
    ) + (jax.ShapeDtypeStruct((B, RES, N_I_PAD, QB_W), _BF16),) * (2 * len(KEY_SHIFTS))
    shifted = pl.BlockSpec((1, tm // N_I, N_I_PAD, QB_W), lambda b, i: (b, i, 0, 0))
    wide = pl.BlockSpec((1, tm, QB_W), lambda b, i: (b, i, 0))
    narrow = pl.BlockSpec((1, tm, LANES), lambda b, i: (b, i, 0))
    return pl.pallas_call(
        _in_kernel,
        grid=(B, nt),
        in_specs=[
            pl.BlockSpec((1, tm, D), lambda b, i: (b, i, 0)),
            pl.BlockSpec((1, D), const),
            _layer_weight(w_in, layer),
            pl.BlockSpec((1, LANES), const),
            pl.BlockSpec((1, LANES), const),
            pl.BlockSpec((LANES, LANES), const),
            tab, tab, tab, tab,
        ],
        out_specs=(
            pl.BlockSpec((1, N_HEADS_A, tm, LANES), lambda b, i: (b, 0, i, 0)),
            narrow,
            pl.BlockSpec((1, N_KV_A, 1, LANES, tm), lambda b, i: (b, 0, i, 0, 0)),
            wide, wide, wide,
        ) + (shifted,) * (2 * len(KEY_SHIFTS)),
        out_shape=out_shape,
        compiler_params=_cparams(2),
        name="in_proj",
    )(x, n1, w_in, qg, kg, gm, *tables)


FAST_MARGIN = 64.0


def _gqa_kernel(q_ref, k_ref, vt_ref, o_ref, acc_ref, *, tk):
    n_q, tq = q_ref.shape[1], q_ref.shape[2]
    cols = n_q * tq
    per_tile = tk // vt_ref.shape[4]
    n_kv = vt_ref.shape[2] // per_tile
    q = q_ref[0].reshape(cols, LANES)

    def scores(i):
        start = i * tk if isinstance(i, int) else pl.multiple_of(i * tk, tk)
        return _dot_nt(k_ref[0, pl.ds(start, tk), :], q)

    def v_tile(i):
        return jnp.concatenate(
            [vt_ref[0, 0, i * per_tile + t] for t in range(per_tile)], axis=1)

    def finish(acc):
        out_t = acc[0:HEAD_DIM] / acc[HEAD_DIM:2 * HEAD_DIM]
        y_t = jnp.concatenate([out_t[:, j * tq:(j + 1) * tq] for j in range(n_q)], axis=0)
        o_ref[0] = y_t.T.astype(o_ref.dtype)

    st = scores(0)
    m0 = jnp.max(st, axis=0, keepdims=True)
    top, acc = m0, None
    for i in range(n_kv):
        st_next = scores(i + 1) if i + 1 < n_kv else None
        if i > 0:
            top = jnp.maximum(top, jnp.max(st, axis=0, keepdims=True))
        pv = _dot(v_tile(i), jnp.exp2(st - m0).astype(_BF16))
        acc = pv if acc is None else acc + pv
        st = st_next
    finish(acc)

    @pl.when(jnp.max(top - m0) > FAST_MARGIN)
    def _():
        acc_ref[...] = jnp.zeros(acc_ref.shape, _F32)

        def body(i, m_prev):
            st = scores(i)
            m_new = jnp.maximum(m_prev, jnp.max(st, axis=0, keepdims=True))
            p = jnp.exp2(st - m_new).astype(_BF16)
            acc_ref[...] = jnp.exp2(m_prev - m_new) * acc_ref[...] + _dot(v_tile(i), p)
            return m_new

        lax.fori_loop(0, n_kv, body, jnp.full((1, cols), NEG_INF, _F32))
        finish(acc_ref[...])


def _gqa(qa, ka, vt, tq, tk):
    B, H, S, _ = qa.shape
    per = H // N_KV_A
    n_kv, tkv = vt.shape[2], vt.shape[4]
    assert tk % tkv == 0
    return pl.pallas_call(
        functools.partial(_gqa_kernel, tk=tk),
        grid=(B, N_KV_A, S // tq),
        in_specs=[
            pl.BlockSpec((1, per, tq, LANES), lambda b, g, i: (b, g, i, 0)),
            pl.BlockSpec((1, S, LANES), lambda b, g, i: (b, 0, 0)),
            pl.BlockSpec((1, 1, n_kv, LANES, tkv), lambda b, g, i: (b, g, 0, 0, 0)),
        ],
        out_specs=pl.BlockSpec((1, tq, per * HEAD_DIM), lambda b, g, i: (b, i, g)),
        out_shape=jax.ShapeDtypeStruct((B, S, QA_W), _BF16),
        scratch_shapes=[pltpu.VMEM((LANES, per * tq), _F32)],
        compiler_params=_cparams(3),
        name="gqa_attn",
    )(qa, ka, vt)


class _Pattern:
    def __init__(self, window, dil, ni_q, unroll, shifts=()):
        self.dil = dil
        self.unroll = unroll
        self.shifts = shifts
        self.n_sub = max(len(shifts), 1)
        self.n_side = window // (2 * dil)
        self.n_rows = RES // dil
        self.ni_q = ni_q
        need = -(-self.n_side // self.n_rows)
        if shifts:
            self.halo = need
            assert ni_q * self.n_sub == BF16_ROWS and (ni_q + 2 * need) % BF16_ROWS == 0
            assert all((h * ni_q - need + s) % BF16_ROWS == 0 and need <= s <= ROW_PAD - need
                       for h, s in enumerate(shifts))
        else:
            self.halo = -(-need // BF16_ROWS) * BF16_ROWS
        self.ni_k = min(ni_q + 2 * self.halo, N_I)
        self.n_blk = N_I // ni_q
        self.variants = sorted({0, min(1, self.n_blk - 1), self.n_blk - 1})

    def key_start(self, b):
        i0 = b * self.ni_q
        return i0 - self.halo if self.shifts else min(max(i0 - self.halo, 0), N_I - self.ni_k)

    def _mask(self, b):
        a_q, i_q = np.divmod(np.arange(self.n_rows * self.ni_q), self.ni_q)
        a_k, i_k = np.divmod(np.arange(self.n_rows * self.ni_k), self.ni_k)
        i_q = i_q + b * self.ni_q
        i_k = i_k + self.key_start(b)
        dj = self.n_rows * (i_q[:, None] - i_k[None, :]) + (a_q[:, None] - a_k[None, :])
        ok = (np.abs(dj) <= self.n_side) & ((i_k >= 0) & (i_k < N_I))[None, :]
        return np.where(ok, 0.0, NEG_INF).astype(np.float32)

    def bias(self):
        for b in range(2, self.n_blk - 1):
            assert np.array_equal(self._mask(b), self._mask(1))
        return jnp.asarray(np.stack([self._mask(b) for b in self.variants]))


RES = 16
N_I = 4096 // RES
BF16_ROWS = 16
ROW_PAD = BF16_ROWS
N_I_PAD = N_I + ROW_PAD
KEY_SHIFTS = (4, 12)
PATTERNS = (_Pattern(128, 1, 8, 4, shifts=KEY_SHIFTS), _Pattern(512, 4, 32, 8),
            _Pattern(2048, 16, N_I, 4))
assert tuple((2 * p.n_side * p.dil, p.dil) for p in PATTERNS) == DILATED_PATTERNS


def _dil_kernel(q_ref, k_ref, v_ref, ks0_ref, vs0_ref, ks1_ref, vs1_ref,
                b0_ref, b1_ref, b2_ref, out_ref, a_s, m_s, l_s):
    bias_refs = (b0_ref, b1_ref, b2_ref)
    shifted_refs = ((ks0_ref, vs0_ref), (ks1_ref, vs1_ref))

    def attend(q, kw, vw, bias):
        nq, nk = q.shape[0], kw.shape[0]
        low = lax.broadcasted_iota(jnp.int32, (nq, LANES), 1) < HEAD_DIM
        vext = jnp.concatenate([vw, jnp.ones((nk, LANES), _BF16)], axis=1)
        zero = jnp.zeros_like(q)
        q2 = jnp.concatenate([jnp.where(low, q, zero), jnp.where(low, zero, q)], axis=0)
        s = _dot_nt(q2, kw) + jnp.concatenate([bias, bias], axis=0)
        m = jnp.max(s, axis=-1, keepdims=True)
        pv = _dot(jnp.exp2(s - m).astype(_BF16), vext)
        mb = jnp.broadcast_to(m, (2 * nq, LANES))
        return tuple(jnp.where(low, t[0:nq], t[nq:])
                     for t in (pv[:, 0:LANES], mb, pv[:, LANES:2 * LANES]))

    for pi, pat in enumerate(PATTERNS):
        n_var = len(pat.variants)
        ni_load = pat.ni_q * pat.n_sub
        n_load = N_I // ni_load

        def body(blk, carry, pi=pi, pat=pat, n_var=n_var, ni_load=ni_load, n_load=n_load):
            rho = blk // n_load
            il = blk % n_load
            i0 = pl.multiple_of(il * ni_load, ni_load)
            rows = [rho + pat.dil * a for a in range(pat.n_rows)]
            qs = [q_ref[0, r, pl.ds(i0, ni_load), :] for r in rows]
            if pat.shifts:
                qs = [c.astype(_F32) for c in qs]
            for h in range(pat.n_sub):
                ib = il * pat.n_sub + h
                iq = pl.multiple_of(i0 + h * pat.ni_q, pat.ni_q)
                if pat.shifts:
                    kr, vr = shifted_refs[h]
                    ws = i0 + (h * pat.ni_q - pat.halo + pat.shifts[h])
                    q = jnp.concatenate(
                        [c[h * pat.ni_q:(h + 1) * pat.ni_q] for c in qs], axis=0).astype(_BF16)
                else:
                    kr, vr = k_ref, v_ref
                    ws = jnp.clip(i0 - pat.halo, 0, N_I - pat.ni_k)
                    q = jnp.concatenate(qs, axis=0)
                ws = pl.multiple_of(ws, BF16_ROWS)
                var = jnp.where(ib == 0, 0, jnp.where(ib == pat.n_blk - 1, n_var - 1, 1)) if n_var > 1 else 0
                kw = jnp.concatenate([kr[0, r, pl.ds(ws, pat.ni_k), :] for r in rows], axis=0)
                vw = jnp.concatenate([vr[0, r, pl.ds(ws, pat.ni_k), :] for r in rows], axis=0)
                acc, mx, den = attend(q, kw, vw, bias_refs[pi][var])
                for a, r in enumerate(rows):
                    sl = slice(a * pat.ni_q, (a + 1) * pat.ni_q)
                    a_s[pi, r, pl.ds(iq, pat.ni_q), :] = acc[sl]
                    m_s[pi, r, pl.ds(iq, pat.ni_q), :] = mx[sl]
                    l_s[pi, r, pl.ds(iq, pat.ni_q), :] = den[sl]
            return carry

        lax.fori_loop(0, pat.dil * n_load, body, 0, unroll=pat.unroll)

    def merge(r, carry):
        ms = [m_s[p, r] for p in range(len(PATTERNS))]
        top = functools.reduce(jnp.maximum, ms)
        es = [jnp.exp2(m - top) for m in ms]
        num = sum(e * a_s[p, r] for p, e in enumerate(es))
        den = sum(e * l_s[p, r] for p, e in enumerate(es))
        out_ref[0, r] = (num / den).astype(out_ref.dtype)
        return carry

    lax.fori_loop(0, RES, merge, 0)


def _dilated(qb, kb, vb, shifted, biases):
    B, S, W = qb.shape
    view = lambda a: a.reshape(B, RES, N_I, W)
    spec = pl.BlockSpec((1, RES, N_I, LANES), lambda b, c: (b, 0, 0, c))
    sspec = pl.BlockSpec((1, RES, N_I_PAD, LANES), lambda b, c: (b, 0, 0, c))
    bspecs = [pl.BlockSpec(bb.shape, lambda b, c: (0, 0, 0)) for bb in biases]
    yb = pl.pallas_call(
        _dil_kernel,
        grid=(B, W // LANES),
        in_specs=[spec, spec, spec] + [sspec] * len(shifted) + bspecs,
        out_specs=spec,
        out_shape=jax.ShapeDtypeStruct((B, RES, N_I, W), _BF16),
        scratch_shapes=[pltpu.VMEM((len(PATTERNS), RES, N_I, LANES), _F32)] * 3,
        compiler_params=_cparams(2),
        name="dilated",
    )(view(qb), view(kb), view(vb), *shifted, *biases)
    return yb.reshape(B, S, W)


def _post_kernel(x_ref, ya_ref, yb_ref, ga_ref, gb_ref, wo_ref, n2_ref, w1_ref, w2_ref,
                 fn_ref, out_ref, *, chunk, final_norm):
    ya, yb = ya_ref[0].astype(_F32), yb_ref[0].astype(_F32)
    ya_n = (ya * _rms_scale(ya) * ga_ref[...]).astype(_BF16)
    yb_n = (yb * _rms_scale(yb) * gb_ref[...]).astype(_BF16)
    x = (x_ref[0] + _dot(ya_n, wo_ref[0:QA_W, :])
         + _dot(yb_n, wo_ref[QA_W:QA_W + QB_W, :]))
    h = (x * _rms_scale(x) * n2_ref[...]).astype(_BF16)
    acc = jnp.zeros_like(x)
    for c in range(D_FF // chunk):
        sl = slice(c * chunk, (c + 1) * chunk)
        u = jnp.maximum(_dot(h, w1_ref[:, sl]), 0.0)
        acc = acc + _dot((u * u).astype(_BF16), w2_ref[sl, :])
    acc = x + acc
    if final_norm:
        acc = acc * _rms_scale(acc) * fn_ref[...]
    out_ref[0] = acc


def _post(x, ya, yb, ga, gb, w_out, n2, w1, w2, layer, fn, tm, final_norm):
    B, S, D = x.shape
    const = lambda b, i: (0, 0)
    half = pl.BlockSpec((1, tm, QA_W), lambda b, i: (b, i, 0))
    full = pl.BlockSpec((1, tm, D), lambda b, i: (b, i, 0))
    return pl.pallas_call(
        functools.partial(_post_kernel, chunk=1024, final_norm=final_norm),
        grid=(B, S // tm),
        in_specs=[full, half, half,
                  pl.BlockSpec((1, QA_W), const), pl.BlockSpec((1, QB_W), const),
                  _layer_weight(w_out, layer),
                  pl.BlockSpec((1, D), const), _layer_weight(w1, layer), _layer_weight(w2, layer),
                  pl.BlockSpec((1, D), const)],
        out_specs=full,
        out_shape=jax.ShapeDtypeStruct((B, S, D), _F32),
        compiler_params=_cparams(2),
        name="post",
    )(x, ya, yb, ga, gb, w_out, n2, w1, w2, fn)


def kernel(x, norm1, w_in, q_norm, k_norm, out_norm_a, out_norm_b, w_out,
           norm2, w_mlp_in, w_mlp_out, final_norm):
    B, S, D = x.shape
    assert S == RES * N_I
    depth = w_in.shape[0]
    to_res = lambda a: jnp.swapaxes(a.reshape(a.shape[:-2] + (N_I, RES, a.shape[-1])), -3, -2
                                    ).reshape(a.shape)
    from_res = lambda a: jnp.swapaxes(a.reshape(a.shape[:-2] + (RES, N_I, a.shape[-1])), -3, -2
                                      ).reshape(a.shape)
    tables = [jnp.asarray(t.reshape(N_I, RES, LANES).swapaxes(0, 1).reshape(S, LANES))
              for t in _rope_tables(S)]
    biases = [p.bias() for p in PATTERNS]
    gm = _group_mean_matrix()
    row = lambda a: a.reshape(1, -1).astype(_F32)
    fn = row(final_norm)
    w_in, w_out, w_mlp_in, w_mlp_out = (w.astype(_BF16) for w in (w_in, w_out, w_mlp_in, w_mlp_out))
    x = to_res(x)
    for l in range(depth):
        qg = row(jnp.tile(q_norm[l], LANES // HEAD_DIM))
        kg = row(jnp.tile(k_norm[l], LANES // HEAD_DIM))
        qa, ka, vt, qb, kb, vb, *shifted = _in_proj(
            x, row(norm1[l]), w_in, l, qg, kg, gm, tables, tm=512)
        ya = _gqa(qa, ka, vt, tq=1024, tk=512)
        yb = _dilated(qb, kb, vb, shifted, biases)
        x = _post(x, ya, yb, row(out_norm_a[l]), row(out_norm_b[l]), w_out,
                  row(norm2[l]), w_mlp_in, w_mlp_out, l,
                  fn, tm=512, final_norm=(l == depth - 1))
    return from_res(x)
```

```python
import functools

import jax
import jax.numpy as jnp
import numpy as np
from jax import lax
from jax.experimental import pallas as pl
from jax.experimental.pallas import tpu as pltpu

D_MODEL = 1024
HEAD_DIM = 64
N_HEADS_A = 8
N_KV_A = 2
N_HEADS_B = 8
D_FF = 4 * D_MODEL
GRID_W = 64
AXIAL_THETA = 10000.0
ROPE_THETA = 500000.0
ROPE_DIM = HEAD_DIM // 4
DILATED_PATTERNS = ((128, 1), (512, 4), (2048, 16))
NORM_EPS = 1e-6
NEG_INF = -1e30

QA_W = N_HEADS_A * HEAD_DIM
KVA_W = N_KV_A * HEAD_DIM
QB_W = N_HEADS_B * HEAD_DIM
IN_W = QA_W + 2 * KVA_W + 3 * QB_W
SCALE_LOG2 = HEAD_DIM ** -0.5 * float(np.log2(np.e))

LANES = 128
BF16_ROWS = 16
VT_ROWS = 64 + BF16_ROWS
VMEM_LIMIT = 56 * 1024 * 1024

_F32 = jnp.float32
_BF16 = jnp.bfloat16


def _cparams(n_grid):
    return pltpu.CompilerParams(
        dimension_semantics=("arbitrary",) * n_grid, vmem_limit_bytes=VMEM_LIMIT)


def _dot(a, b):
    return jnp.dot(a, b, preferred_element_type=_F32)


def _dot_nt(a, b):
    return lax.dot_general(a, b, (((1,), (1,)), ((), ())), preferred_element_type=_F32)


def _rms_scale(x):
    return lax.rsqrt(jnp.mean(x * x, axis=-1, keepdims=True) + NORM_EPS)


def _rope_tables(seq):
    e = np.arange(LANES) % HEAD_DIM
    t = np.arange(seq, dtype=np.float64)
    half = HEAD_DIM // 4
    freqs = AXIAL_THETA ** (-np.arange(half, dtype=np.float64) / half)
    pos = np.where((e < HEAD_DIM // 2)[None, :], (t // GRID_W)[:, None], (t % GRID_W)[:, None])
    ang = pos * freqs[e % half][None, :]
    sign = np.where((e % (2 * half)) < half, -1.0, 1.0)
    ax_cos = np.cos(ang)
    ax_sin = np.sin(ang) * sign[None, :]
    halfp = ROPE_DIM // 2
    freqs_p = ROPE_THETA ** (-np.arange(halfp, dtype=np.float64) / halfp)
    angp = t[:, None] * freqs_p[e % halfp][None, :]
    rot = (e < ROPE_DIM)[None, :]
    signp = np.where(e < halfp, -1.0, 1.0)
    pr_cos = np.where(rot, np.cos(angp), 1.0)
    pr_sin = np.where(rot, np.sin(angp) * signp[None, :], 0.0)
    return tuple(a.astype(np.float32) for a in (ax_cos, ax_sin, pr_cos, pr_sin))


def _group_mean_matrix():
    g = np.arange(LANES) // HEAD_DIM
    return jnp.asarray((g[:, None] == g[None, :]).astype(np.float32) / HEAD_DIM, dtype=_BF16)


def _in_body(x, n1_ref, w_ref, qg_ref, kg_ref, gm_ref,
             axc_ref, axs_ref, prc_ref, prs_ref,
             qa_ref, ka_ref, vt_ref, qb_ref, kb_ref, vb_ref,
             ks0_ref, vs0_ref, ks1_ref, vs1_ref):
    h = (x * _rms_scale(x) * n1_ref[...]).astype(_BF16)
    tm = x.shape[0]
    lane = lax.broadcasted_iota(jnp.int32, (tm, LANES), 1)
    e = lane % HEAD_DIM
    low = lane < HEAD_DIM
    gm = gm_ref[...]

    def head_norm(v, gain):
        sq = v * v
        hi = sq.astype(_BF16)
        lo = (sq - hi.astype(_F32)).astype(_BF16)
        ms = _dot(hi, gm) + _dot(lo, gm)
        return v * lax.rsqrt(ms + NORM_EPS) * gain

    def axial(v):
        first = (e % 32) < 16
        partner = jnp.where(first, pltpu.roll(v, LANES - 16, 1), pltpu.roll(v, 16, 1))
        return v * axc_ref[...] + partner * axs_ref[...]

    def partial(v):
        partner = jnp.where(e < 8, pltpu.roll(v, LANES - 8, 1), pltpu.roll(v, 8, 1))
        return v * prc_ref[...] + partner * prs_ref[...]

    pa = _dot(h, w_ref[:, 0:QA_W])
    for c in range(QA_W // LANES):
        v = axial(head_norm(pa[:, c * LANES:(c + 1) * LANES], qg_ref[...])) * SCALE_LOG2
        swapped = pltpu.roll(v, HEAD_DIM, 1)
        for j in range(2):
            hd = 2 * c + j
            grp = hd // (N_HEADS_A // N_KV_A)
            src = v if j == grp else swapped
            keep = low if grp == 0 else jnp.logical_not(low)
            qa_ref[0, hd] = jnp.where(keep, src, 0.0).astype(_BF16)

    pkv = _dot(h, w_ref[:, QA_W:QA_W + 2 * KVA_W])
    ka_ref[0] = axial(head_norm(pkv[:, 0:LANES], kg_ref[...])).astype(_BF16)
    vt = pkv[:, LANES:2 * LANES].T
    ones = jnp.ones((VT_ROWS - HEAD_DIM, tm), _F32)
    for g in range(N_KV_A):
        vg = vt[g * HEAD_DIM:(g + 1) * HEAD_DIM]
        vt_ref[0, g, 0] = jnp.concatenate([vg, ones], axis=0).astype(_BF16)

    def store_shifted(refs, sl, v):
        half = ROW_PAD // 2
        pad = jnp.zeros((half, v.shape[1]), _F32)
        for r in range(tm // N_I):
            centred = jnp.concatenate([pad, v[r * N_I:(r + 1) * N_I], pad], axis=0)
            for ref, shift in zip(refs, KEY_SHIFTS):
                ref[0, r, :, sl] = pltpu.roll(
                    centred, (shift - half) % N_I_PAD, 0).astype(_BF16)

    o = QA_W + 2 * KVA_W
    pq = _dot(h, w_ref[:, o:o + QB_W])
    pk = _dot(h, w_ref[:, o + QB_W:o + 2 * QB_W])
    for c in range(QB_W // LANES):
        sl = slice(c * LANES, (c + 1) * LANES)
        qb_ref[0, :, sl] = (partial(pq[:, sl]) * SCALE_LOG2).astype(_BF16)
        kc = partial(pk[:, sl])
        kb_ref[0, :, sl] = kc.astype(_BF16)
        store_shifted((ks0_ref, ks1_ref), sl, kc)
    pv = _dot(h, w_ref[:, o + 2 * QB_W:o + 3 * QB_W])
    vb_ref[0] = pv.astype(_BF16)
    for c in range(QB_W // LANES):
        sl = slice(c * LANES, (c + 1) * LANES)
        store_shifted((vs0_ref, vs1_ref), sl, pv[:, sl])


def _layer_weight(w, layer):
    return pl.BlockSpec((None,) + w.shape[1:], lambda b, i: (layer, 0, 0),
                        pipeline_mode=pl.Buffered(1))


def _in_kernel(x_ref, *refs):
    _in_body(x_ref[0], *refs)


def _in_param_specs(w_in, layer, tm):
    D = w_in.shape[1]
    const = lambda b, i: (0, 0)
    tab = pl.BlockSpec((tm, LANES), lambda b, i: (i, 0))
    return [pl.BlockSpec((1, D), const), _layer_weight(w_in, layer),
            pl.BlockSpec((1, LANES), const), pl.BlockSpec((1, LANES), const),
            pl.BlockSpec((LANES, LANES), const), tab, tab, tab, tab]


def _in_out_specs(B, S, tm):
    assert tm % N_I == 0
    nt = S // tm
    shapes = (
        jax.ShapeDtypeStruct((B, N_HEADS_A, S, LANES), _BF16),
        jax.ShapeDtypeStruct((B, S, LANES), _BF16),
        jax.ShapeDtypeStruct((B, N_KV_A, nt, VT_ROWS, tm), _BF16),
        jax.ShapeDtypeStruct((B, S, QB_W), _BF16),
        jax.ShapeDtypeStruct((B, S, QB_W), _BF16),
        jax.ShapeDtypeStruct((B, S, QB_W), _BF16),
    ) + (jax.ShapeDtypeStruct((B, RES, N_I_PAD, QB_W), _BF16),) * (2 * len(KEY_SHIFTS))
    shifted = pl.BlockSpec((1, tm // N_I, N_I_PAD, QB_W), lambda b, i: (b, i, 0, 0))
    wide = pl.BlockSpec((1, tm, QB_W), lambda b, i: (b, i, 0))
    narrow = pl.BlockSpec((1, tm, LANES), lambda b, i: (b, i, 0))
    specs = (
        pl.BlockSpec((1, N_HEADS_A, tm, LANES), lambda b, i: (b, 0, i, 0)),
        narrow,
        pl.BlockSpec((1, N_KV_A, 1, VT_ROWS, tm), lambda b, i: (b, 0, i, 0, 0)),
        wide, wide, wide,
    ) + (shifted,) * (2 * len(KEY_SHIFTS))
    return shapes, specs


def _in_proj(x, in_params, w_in, layer, tm):
    B, S, D = x.shape
    shapes, specs = _in_out_specs(B, S, tm)
    return pl.pallas_call(
        _in_kernel,
        grid=(B, S // tm),
        in_specs=[pl.BlockSpec((1, tm, D), lambda b, i: (b, i, 0))]
        + _in_param_specs(w_in, layer, tm),
        out_specs=specs,
        out_shape=shapes,
        compiler_params=_cparams(2),
        name="in_proj",
    )(x, *in_params)


FAST_MARGIN = 64.0


def _gqa_kernel(q_ref, k_ref, vt_ref, o_ref, acc_ref, *, tk):
    n_q, tq = q_ref.shape[1], q_ref.shape[2]
    cols = n_q * tq
    per_tile = tk // vt_ref.shape[4]
    n_kv = vt_ref.shape[2] // per_tile
    q = q_ref[0].reshape(cols, LANES)

    def scores(i):
        start = i * tk if isinstance(i, int) else pl.multiple_of(i * tk, tk)
        return _dot_nt(k_ref[0, pl.ds(start, tk), :], q)

    def v_tile(i):
        return jnp.concatenate(
            [vt_ref[0, 0, i * per_tile + t] for t in range(per_tile)], axis=1)

    def finish(acc):
        out_t = acc[0:HEAD_DIM] / acc[HEAD_DIM:HEAD_DIM + 1]
        y_t = jnp.concatenate([out_t[:, j * tq:(j + 1) * tq] for j in range(n_q)], axis=0)
        o_ref[0] = y_t.T.astype(o_ref.dtype)

    st = scores(0)
    m0 = jnp.max(st, axis=0, keepdims=True)
    top, acc = m0, None
    for i in range(n_kv):
        st_next = scores(i + 1) if i + 1 < n_kv else None
        if i > 0:
            top = jnp.maximum(top, jnp.max(st, axis=0, keepdims=True))
        pv = _dot(v_tile(i), jnp.exp2(st - m0).astype(_BF16))
        acc = pv if acc is None else acc + pv
        st = st_next
    finish(acc)

    @pl.when(jnp.max(top - m0) > FAST_MARGIN)
    def _():
        acc_ref[...] = jnp.zeros(acc_ref.shape, _F32)

        def body(i, m_prev):
            st = scores(i)
            m_new = jnp.maximum(m_prev, jnp.max(st, axis=0, keepdims=True))
            p = jnp.exp2(st - m_new).astype(_BF16)
            acc_ref[...] = jnp.exp2(m_prev - m_new) * acc_ref[...] + _dot(v_tile(i), p)
            return m_new

        lax.fori_loop(0, n_kv, body, jnp.full((1, cols), NEG_INF, _F32))
        finish(acc_ref[...])


def _gqa(qa, ka, vt, tq, tk):
    B, H, S, _ = qa.shape
    per = H // N_KV_A
    n_kv, tkv = vt.shape[2], vt.shape[4]
    assert tk % tkv == 0
    return pl.pallas_call(
        functools.partial(_gqa_kernel, tk=tk),
        grid=(B, N_KV_A, S // tq),
        in_specs=[
            pl.BlockSpec((1, per, tq, LANES), lambda b, g, i: (b, g, i, 0)),
            pl.BlockSpec((1, S, LANES), lambda b, g, i: (b, 0, 0)),
            pl.BlockSpec((1, 1, n_kv, VT_ROWS, tkv), lambda b, g, i: (b, g, 0, 0, 0)),
        ],
        out_specs=pl.BlockSpec((1, tq, per * HEAD_DIM), lambda b, g, i: (b, i, g)),
        out_shape=jax.ShapeDtypeStruct((B, S, QA_W), _BF16),
        scratch_shapes=[pltpu.VMEM((VT_ROWS, per * tq), _F32)],
        compiler_params=_cparams(3),
        name="gqa_attn",
    )(qa, ka, vt)


class _Pattern:
    def __init__(self, window, dil, ni_q, unroll, shifts=()):
        self.dil = dil
        self.unroll = unroll
        self.shifts = shifts
        self.n_sub = max(len(shifts), 1)
        self.n_side = window // (2 * dil)
        self.n_rows = RES // dil
        self.ni_q = ni_q
        need = -(-self.n_side // self.n_rows)
        if shifts:
            self.halo = need
            assert ni_q * self.n_sub == BF16_ROWS and (ni_q + 2 * need) % BF16_ROWS == 0
            assert all((h * ni_q - need + s) % BF16_ROWS == 0 and need <= s <= ROW_PAD - need
                       for h, s in enumerate(shifts))
        else:
            self.halo = -(-need // BF16_ROWS) * BF16_ROWS
        self.ni_k = min(ni_q + 2 * self.halo, N_I)
        self.n_blk = N_I // ni_q
        self.variants = sorted({0, min(1, self.n_blk - 1), self.n_blk - 1})

    def key_start(self, b):
        i0 = b * self.ni_q
        return i0 - self.halo if self.shifts else min(max(i0 - self.halo, 0), N_I - self.ni_k)

    def _mask(self, b):
        a_q, i_q = np.divmod(np.arange(self.n_rows * self.ni_q), self.ni_q)
        a_k, i_k = np.divmod(np.arange(self.n_rows * self.ni_k), self.ni_k)
        i_q = i_q + b * self.ni_q
        i_k = i_k + self.key_start(b)
        dj = self.n_rows * (i_q[:, None] - i_k[None, :]) + (a_q[:, None] - a_k[None, :])
        ok = (np.abs(dj) <= self.n_side) & ((i_k >= 0) & (i_k < N_I))[None, :]
        return np.where(ok, 0.0, NEG_INF).astype(np.float32)

    def bias(self):
        for b in range(2, self.n_blk - 1):
            assert np.array_equal(self._mask(b), self._mask(1))
        return jnp.asarray(np.stack([self._mask(b) for b in self.variants]))


RES = 16
N_I = 4096 // RES
ROW_PAD = BF16_ROWS
N_I_PAD = N_I + ROW_PAD
KEY_SHIFTS = (4, 12)
PATTERNS = (_Pattern(128, 1, 8, 16, shifts=KEY_SHIFTS), _Pattern(512, 4, 32, 32),
            _Pattern(2048, 16, N_I, 16))
assert tuple((2 * p.n_side * p.dil, p.dil) for p in PATTERNS) == DILATED_PATTERNS


def _dil_kernel(q_ref, k_ref, v_ref, ks0_ref, vs0_ref, ks1_ref, vs1_ref,
                b0_ref, b1_ref, b2_ref, out_ref, a_s, m_s, l_s):
    bias_refs = (b0_ref, b1_ref, b2_ref)
    shifted_refs = ((ks0_ref, vs0_ref), (ks1_ref, vs1_ref))
    n_stored = len(PATTERNS) - 1
    assert PATTERNS[-1].n_rows == 1 and PATTERNS[-1].ni_q == N_I and not PATTERNS[-1].shifts

    def attend(q, kw, vw, bias):
        nq, nk = q.shape[0], kw.shape[0]
        low = lax.broadcasted_iota(jnp.int32, (nq, LANES), 1) < HEAD_DIM
        vext = jnp.concatenate([vw, jnp.ones((nk, LANES), _BF16)], axis=1)
        zero = jnp.zeros_like(q)
        q2 = jnp.concatenate([jnp.where(low, q, zero), jnp.where(low, zero, q)], axis=0)
        s = _dot_nt(q2, kw) + jnp.concatenate([bias, bias], axis=0)
        m = jnp.max(s, axis=-1, keepdims=True)
        pv = _dot(jnp.exp2(s - m).astype(_BF16), vext)
        mb = jnp.broadcast_to(m, (2 * nq, LANES))
        return tuple(jnp.where(low, t[0:nq], t[nq:])
                     for t in (pv[:, 0:LANES], mb, pv[:, LANES:2 * LANES]))

    for pi, pat in enumerate(PATTERNS):
        n_var = len(pat.variants)
        ni_load = pat.ni_q * pat.n_sub
        n_load = N_I // ni_load

        def body(blk, carry, pi=pi, pat=pat, n_var=n_var, ni_load=ni_load, n_load=n_load):
            rho = blk // n_load
            il = blk % n_load
            i0 = pl.multiple_of(il * ni_load, ni_load)
            rows = [rho + pat.dil * a for a in range(pat.n_rows)]
            qs = [q_ref[0, r, pl.ds(i0, ni_load), :] for r in rows]
            if pat.shifts:
                qs = [c.astype(_F32) for c in qs]
            for h in range(pat.n_sub):
                ib = il * pat.n_sub + h
                iq = pl.multiple_of(i0 + h * pat.ni_q, pat.ni_q)
                if pat.shifts:
                    kr, vr = shifted_refs[h]
                    ws = i0 + (h * pat.ni_q - pat.halo + pat.shifts[h])
                    q = jnp.concatenate(
                        [c[h * pat.ni_q:(h + 1) * pat.ni_q] for c in qs], axis=0).astype(_BF16)
                else:
                    kr, vr = k_ref, v_ref
                    ws = jnp.clip(i0 - pat.halo, 0, N_I - pat.ni_k)
                    q = jnp.concatenate(qs, axis=0)
                ws = pl.multiple_of(ws, BF16_ROWS)
                var = jnp.where(ib == 0, 0, jnp.where(ib == pat.n_blk - 1, n_var - 1, 1)) if n_var > 1 else 0
                kw = jnp.concatenate([kr[0, r, pl.ds(ws, pat.ni_k), :] for r in rows], axis=0)
                vw = jnp.concatenate([vr[0, r, pl.ds(ws, pat.ni_k), :] for r in rows], axis=0)
                acc, mx, den = attend(q, kw, vw, bias_refs[pi][var])
                if pi == n_stored:
                    parts = [(a_s[p, rho], m_s[p, rho], l_s[p, rho]) for p in range(n_stored)]
                    parts.append((acc, mx, den))
                    top = functools.reduce(jnp.maximum, [m for _, m, _ in parts])
                    es = [jnp.exp2(m - top) for _, m, _ in parts]
                    num = sum(e * a for e, (a, _, _) in zip(es, parts))
                    tot = sum(e * l for e, (_, _, l) in zip(es, parts))
                    out_ref[0, rho] = (num / tot).astype(out_ref.dtype)
                    continue
                for a, r in enumerate(rows):
                    sl = slice(a * pat.ni_q, (a + 1) * pat.ni_q)
                    a_s[pi, r, pl.ds(iq, pat.ni_q), :] = acc[sl]
                    m_s[pi, r, pl.ds(iq, pat.ni_q), :] = mx[sl]
                    l_s[pi, r, pl.ds(iq, pat.ni_q), :] = den[sl]
            return carry

        lax.fori_loop(0, pat.dil * n_load, body, 0, unroll=pat.unroll)


def _dilated(qb, kb, vb, shifted, biases):
    B, S, W = qb.shape
    view = lambda a: a.reshape(B, RES, N_I, W)
    spec = pl.BlockSpec((1, RES, N_I, LANES), lambda b, c: (b, 0, 0, c))
    sspec = pl.BlockSpec((1, RES, N_I_PAD, LANES), lambda b, c: (b, 0, 0, c))
    bspecs = [pl.BlockSpec(bb.shape, lambda b, c: (0, 0, 0)) for bb in biases]
    yb = pl.pallas_call(
        _dil_kernel,
        grid=(B, W // LANES),
        in_specs=[spec, spec, spec] + [sspec] * len(shifted) + bspecs,
        out_specs=spec,
        out_shape=jax.ShapeDtypeStruct((B, RES, N_I, W), _BF16),
        scratch_shapes=[pltpu.VMEM((len(PATTERNS) - 1, RES, N_I, LANES), _F32)] * 3,
        compiler_params=_cparams(2),
        name="dilated",
    )(view(qb), view(kb), view(vb), *shifted, *biases)
    return yb.reshape(B, S, W)


def _post_kernel(x_ref, ya_ref, yb_ref, ga_ref, gb_ref, wo_ref, n2_ref, w1_ref, w2_ref,
                 fn_ref, out_ref, *, chunk, final_norm):
    ya, yb = ya_ref[0].astype(_F32), yb_ref[0].astype(_F32)
    ya_n = (ya * _rms_scale(ya) * ga_ref[...]).astype(_BF16)
    yb_n = (yb * _rms_scale(yb) * gb_ref[...]).astype(_BF16)
    x = (x_ref[0] + _dot(ya_n, wo_ref[0:QA_W, :])
         + _dot(yb_n, wo_ref[QA_W:QA_W + QB_W, :]))
    h = (x * _rms_scale(x) * n2_ref[...]).astype(_BF16)
    acc = jnp.zeros_like(x)
    for c in range(D_FF // chunk):
        sl = slice(c * chunk, (c + 1) * chunk)
        u = jnp.maximum(_dot(h, w1_ref[:, sl]), 0.0)
        acc = acc + _dot((u * u).astype(_BF16), w2_ref[sl, :])
    acc = x + acc
    if final_norm:
        acc = acc * _rms_scale(acc) * fn_ref[...]
    out_ref[0] = acc


def _post(x, ya, yb, ga, gb, w_out, n2, w1, w2, layer, fn, tm, final_norm):
    B, S, D = x.shape
    const = lambda b, i: (0, 0)
    half = pl.BlockSpec((1, tm, QA_W), lambda b, i: (b, i, 0))
    full = pl.BlockSpec((1, tm, D), lambda b, i: (b, i, 0))
    return pl.pallas_call(
        functools.partial(_post_kernel, chunk=1024, final_norm=final_norm),
        grid=(B, S // tm),
        in_specs=[full, half, half,
                  pl.BlockSpec((1, QA_W), const), pl.BlockSpec((1, QB_W), const),
                  _layer_weight(w_out, layer),
                  pl.BlockSpec((1, D), const), _layer_weight(w1, layer), _layer_weight(w2, layer),
                  pl.BlockSpec((1, D), const)],
        out_specs=full,
        out_shape=jax.ShapeDtypeStruct((B, S, D), _F32),
        compiler_params=_cparams(2),
        name="post",
    )(x, ya, yb, ga, gb, w_out, n2, w1, w2, fn)


def kernel(x, norm1, w_in, q_norm, k_norm, out_norm_a, out_norm_b, w_out,
           norm2, w_mlp_in, w_mlp_out, final_norm):
    B, S, D = x.shape
    assert S == RES * N_I
    depth = w_in.shape[0]
    to_res = lambda a: jnp.swapaxes(a.reshape(a.shape[:-2] + (N_I, RES, a.shape[-1])), -3, -2
                                    ).reshape(a.shape)
    from_res = lambda a: jnp.swapaxes(a.reshape(a.shape[:-2] + (RES, N_I, a.shape[-1])), -3, -2
                                      ).reshape(a.shape)
    tables = [jnp.asarray(t.reshape(N_I, RES, LANES).swapaxes(0, 1).reshape(S, LANES))
              for t in _rope_tables(S)]
    biases = [p.bias() for p in PATTERNS]
    gm = _group_mean_matrix()
    row = lambda a: a.reshape(1, -1).astype(_F32)
    fn = row(final_norm)
    w_in, w_out, w_mlp_in, w_mlp_out = (w.astype(_BF16) for w in (w_in, w_out, w_mlp_in, w_mlp_out))
    x = to_res(x)

    def in_params(l):
        head_gain = lambda g: row(jnp.tile(g, LANES // HEAD_DIM))
        return (row(norm1[l]), w_in, head_gain(q_norm[l]), head_gain(k_norm[l]), gm, *tables)

    for l in range(depth):
        qa, ka, vt, qb, kb, vb, *shifted = _in_proj(x, in_params(l), w_in, l, tm=512)
        ya = _gqa(qa, ka, vt, tq=1024, tk=512)
        yb = _dilated(qb, kb, vb, shifted, biases)
        x = _post(x, ya, yb, row(out_norm_a[l]), row(out_norm_b[l]), w_out,
                  row(norm2[l]), w_mlp_in, w_mlp_out, l, fn, tm=512,
                  final_norm=(l == depth - 1))
    return from_res(x)
```

```python
import functools

import jax
import jax.numpy as jnp
import numpy as np
from jax import lax
from jax.experimental import pallas as pl
from jax.experimental.pallas import tpu as pltpu

D_MODEL = 1024
HEAD_DIM = 64
N_HEADS_A = 8
N_KV_A = 2
N_HEADS_B = 8
D_FF = 4 * D_MODEL
GRID_W = 64
AXIAL_THETA = 10000.0
ROPE_THETA = 500000.0
ROPE_DIM = HEAD_DIM // 4
DILATED_PATTERNS = ((128, 1), (512, 4), (2048, 16))
NORM_EPS = 1e-6
NEG_INF = -1e30

QA_W = N_HEADS_A * HEAD_DIM
KVA_W = N_KV_A * HEAD_DIM
QB_W = N_HEADS_B * HEAD_DIM
IN_W = QA_W + 2 * KVA_W + 3 * QB_W
SCALE_LOG2 = HEAD_DIM ** -0.5 * float(np.log2(np.e))

LANES = 128
BF16_ROWS = 16
VT_ROWS = 2 * 64
VMEM_LIMIT = 56 * 1024 * 1024

_F32 = jnp.float32
_BF16 = jnp.bfloat16


def _cparams(n_grid):
    return pltpu.CompilerParams(
        dimension_semantics=("arbitrary",) * n_grid, vmem_limit_bytes=VMEM_LIMIT)


def _dot(a, b):
    return jnp.dot(a, b, preferred_element_type=_F32)


def _dot_nt(a, b):
    return lax.dot_general(a, b, (((1,), (1,)), ((), ())), preferred_element_type=_F32)


def _rms_scale(x):
    return lax.rsqrt(jnp.mean(x * x, axis=-1, keepdims=True) + NORM_EPS)


def _rope_tables(seq):
    e = np.arange(LANES) % HEAD_DIM
    t = np.arange(seq, dtype=np.float64)
    half = HEAD_DIM // 4
    freqs = AXIAL_THETA ** (-np.arange(half, dtype=np.float64) / half)
    pos = np.where((e < HEAD_DIM // 2)[None, :], (t // GRID_W)[:, None], (t % GRID_W)[:, None])
    ang = pos * freqs[e % half][None, :]
    sign = np.where((e % (2 * half)) < half, -1.0, 1.0)
    ax_cos = np.cos(ang)
    ax_sin = np.sin(ang) * sign[None, :]
    halfp = ROPE_DIM // 2
    freqs_p = ROPE_THETA ** (-np.arange(halfp, dtype=np.float64) / halfp)
    angp = t[:, None] * freqs_p[e % halfp][None, :]
    rot = (e < ROPE_DIM)[None, :]
    signp = np.where(e < halfp, -1.0, 1.0)
    pr_cos = np.where(rot, np.cos(angp), 1.0)
    pr_sin = np.where(rot, np.sin(angp) * signp[None, :], 0.0)
    return tuple(a.astype(np.float32) for a in (ax_cos, ax_sin, pr_cos, pr_sin))


def _group_mean_matrix():
    g = np.arange(LANES) // HEAD_DIM
    return jnp.asarray((g[:, None] == g[None, :]).astype(np.float32) / HEAD_DIM, dtype=_BF16)


def _in_body(x, n1_ref, w_ref, qg_ref, kg_ref, gm_ref,
             axc_ref, axs_ref, prc_ref, prs_ref,
             qa_ref, ka_ref, vt_ref, qb_ref, kb_ref, vb_ref,
             ks0_ref, vs0_ref, ks1_ref, vs1_ref):
    h = (x * _rms_scale(x) * n1_ref[...]).astype(_BF16)
    tm = x.shape[0]
    lane = lax.broadcasted_iota(jnp.int32, (tm, LANES), 1)
    e = lane % HEAD_DIM
    low = lane < HEAD_DIM
    gm = gm_ref[...]

    def head_norm(v, gain):
        sq = v * v
        hi = sq.astype(_BF16)
        lo = (sq - hi.astype(_F32)).astype(_BF16)
        ms = _dot(hi, gm) + _dot(lo, gm)
        return v * lax.rsqrt(ms + NORM_EPS) * gain

    def axial(v):
        first = (e % 32) < 16
        partner = jnp.where(first, pltpu.roll(v, LANES - 16, 1), pltpu.roll(v, 16, 1))
        return v * axc_ref[...] + partner * axs_ref[...]

    def partial(v):
        partner = jnp.where(e < 8, pltpu.roll(v, LANES - 8, 1), pltpu.roll(v, 8, 1))
        return v * prc_ref[...] + partner * prs_ref[...]

    pa = _dot(h, w_ref[:, 0:QA_W])
    for c in range(QA_W // LANES):
        v = axial(head_norm(pa[:, c * LANES:(c + 1) * LANES], qg_ref[...])) * SCALE_LOG2
        swapped = pltpu.roll(v, HEAD_DIM, 1)
        for j in range(2):
            hd = 2 * c + j
            grp = hd // (N_HEADS_A // N_KV_A)
            src = v if j == grp else swapped
            keep = low if grp == 0 else jnp.logical_not(low)
            qa_ref[0, hd] = jnp.where(keep, src, 0.0).astype(_BF16)

    pkv = _dot(h, w_ref[:, QA_W:QA_W + 2 * KVA_W])
    ka_ref[0] = axial(head_norm(pkv[:, 0:LANES], kg_ref[...])).astype(_BF16)
    vt = pkv[:, LANES:2 * LANES].T
    ones = jnp.ones((VT_ROWS - HEAD_DIM, tm), _F32)
    for g in range(N_KV_A):
        vg = vt[g * HEAD_DIM:(g + 1) * HEAD_DIM]
        vt_ref[0, g, 0] = jnp.concatenate([vg, ones], axis=0).astype(_BF16)

    def store_shifted(refs, sl, v):
        half = ROW_PAD // 2
        pad = jnp.zeros((half, v.shape[1]), _F32)
        for r in range(tm // N_I):
            centred = jnp.concatenate([pad, v[r * N_I:(r + 1) * N_I], pad], axis=0)
            for ref, shift in zip(refs, KEY_SHIFTS):
                ref[0, r, :, sl] = pltpu.roll(
                    centred, (shift - half) % N_I_PAD, 0).astype(_BF16)

    o = QA_W + 2 * KVA_W
    pq = _dot(h, w_ref[:, o:o + QB_W])
    pk = _dot(h, w_ref[:, o + QB_W:o + 2 * QB_W])
    for c in range(QB_W // LANES):
        sl = slice(c * LANES, (c + 1) * LANES)
        qb_ref[0, :, sl] = (partial(pq[:, sl]) * SCALE_LOG2).astype(_BF16)
        kc = partial(pk[:, sl])
        kb_ref[0, :, sl] = kc.astype(_BF16)
        store_shifted((ks0_ref, ks1_ref), sl, kc)
    pv = _dot(h, w_ref[:, o + 2 * QB_W:o + 3 * QB_W])
    vb_ref[0] = pv.astype(_BF16)
    for c in range(QB_W // LANES):
        sl = slice(c * LANES, (c + 1) * LANES)
        store_shifted((vs0_ref, vs1_ref), sl, pv[:, sl])


def _layer_weight(w, layer):
    return pl.BlockSpec((None,) + w.shape[1:], lambda b, i: (layer, 0, 0),
                        pipeline_mode=pl.Buffered(1))


def _in_kernel(x_ref, *refs):
    _in_body(x_ref[0], *refs)


def _in_param_specs(w_in, layer, tm):
    D = w_in.shape[1]
    const = lambda b, i: (0, 0)
    tab = pl.BlockSpec((tm, LANES), lambda b, i: (i, 0))
    return [pl.BlockSpec((1, D), const), _layer_weight(w_in, layer),
            pl.BlockSpec((1, LANES), const), pl.BlockSpec((1, LANES), const),
            pl.BlockSpec((LANES, LANES), const), tab, tab, tab, tab]


def _in_out_specs(B, S, tm):
    assert tm % N_I == 0
    nt = S // tm
    shapes = (
        jax.ShapeDtypeStruct((B, N_HEADS_A, S, LANES), _BF16),
        jax.ShapeDtypeStruct((B, S, LANES), _BF16),
        jax.ShapeDtypeStruct((B, N_KV_A, nt, VT_ROWS, tm), _BF16),
        jax.ShapeDtypeStruct((B, S, QB_W), _BF16),
        jax.ShapeDtypeStruct((B, S, QB_W), _BF16),
        jax.ShapeDtypeStruct((B, S, QB_W), _BF16),
    ) + (jax.ShapeDtypeStruct((B, RES, N_I_PAD, QB_W), _BF16),) * (2 * len(KEY_SHIFTS))
    shifted = pl.BlockSpec((1, tm // N_I, N_I_PAD, QB_W), lambda b, i: (b, i, 0, 0))
    wide = pl.BlockSpec((1, tm, QB_W), lambda b, i: (b, i, 0))
    narrow = pl.BlockSpec((1, tm, LANES), lambda b, i: (b, i, 0))
    specs = (
        pl.BlockSpec((1, N_HEADS_A, tm, LANES), lambda b, i: (b, 0, i, 0)),
        narrow,
        pl.BlockSpec((1, N_KV_A, 1, VT_ROWS, tm), lambda b, i: (b, 0, i, 0, 0)),
        wide, wide, wide,
    ) + (shifted,) * (2 * len(KEY_SHIFTS))
    return shapes, specs


def _in_proj(x, in_params, w_in, layer, tm):
    B, S, D = x.shape
    shapes, specs = _in_out_specs(B, S, tm)
    return pl.pallas_call(
        _in_kernel,
        grid=(B, S // tm),
        in_specs=[pl.BlockSpec((1, tm, D), lambda b, i: (b, i, 0))]
        + _in_param_specs(w_in, layer, tm),
        out_specs=specs,
        out_shape=shapes,
        compiler_params=_cparams(2),
        name="in_proj",
    )(x, *in_params)


FAST_MARGIN = 64.0


def _gqa_kernel(q_ref, k_ref, vt_ref, o_ref, acc_ref, *, tk):
    n_q, tq = q_ref.shape[1], q_ref.shape[2]
    cols = n_q * tq
    per_tile = tk // vt_ref.shape[4]
    n_kv = vt_ref.shape[2] // per_tile
    q = q_ref[0].reshape(cols, LANES)

    def scores(i):
        start = i * tk if isinstance(i, int) else pl.multiple_of(i * tk, tk)
        return _dot_nt(k_ref[0, pl.ds(start, tk), :], q)

    def v_tile(i):
        return jnp.concatenate(
            [vt_ref[0, 0, i * per_tile + t] for t in range(per_tile)], axis=1)

    def finish(acc):
        out_t = acc[0:HEAD_DIM] / acc[HEAD_DIM:2 * HEAD_DIM]
        y_t = jnp.concatenate([out_t[:, j * tq:(j + 1) * tq] for j in range(n_q)], axis=0)
        o_ref[0] = y_t.T.astype(o_ref.dtype)

    st = scores(0)
    m0 = jnp.max(st, axis=0, keepdims=True)
    top, acc = m0, None
    for i in range(n_kv):
        st_next = scores(i + 1) if i + 1 < n_kv else None
        if i > 0:
            top = jnp.maximum(top, jnp.max(st, axis=0, keepdims=True))
        pv = _dot(v_tile(i), jnp.exp2(st - m0).astype(_BF16))
        acc = pv if acc is None else acc + pv
        st = st_next
    finish(acc)

    @pl.when(jnp.max(top - m0) > FAST_MARGIN)
    def _():
        acc_ref[...] = jnp.zeros(acc_ref.shape, _F32)

        def body(i, m_prev):
            st = scores(i)
            m_new = jnp.maximum(m_prev, jnp.max(st, axis=0, keepdims=True))
            p = jnp.exp2(st - m_new).astype(_BF16)
            acc_ref[...] = jnp.exp2(m_prev - m_new) * acc_ref[...] + _dot(v_tile(i), p)
            return m_new

        lax.fori_loop(0, n_kv, body, jnp.full((1, cols), NEG_INF, _F32))
        finish(acc_ref[...])


def _gqa(qa, ka, vt, tq, tk):
    B, H, S, _ = qa.shape
    per = H // N_KV_A
    n_kv, tkv = vt.shape[2], vt.shape[4]
    assert tk % tkv == 0
    return pl.pallas_call(
        functools.partial(_gqa_kernel, tk=tk),
        grid=(B, N_KV_A, S // tq),
        in_specs=[
            pl.BlockSpec((1, per, tq, LANES), lambda b, g, i: (b, g, i, 0)),
            pl.BlockSpec((1, S, LANES), lambda b, g, i: (b, 0, 0)),
            pl.BlockSpec((1, 1, n_kv, VT_ROWS, tkv), lambda b, g, i: (b, g, 0, 0, 0)),
        ],
        out_specs=pl.BlockSpec((1, tq, per * HEAD_DIM), lambda b, g, i: (b, i, g)),
        out_shape=jax.ShapeDtypeStruct((B, S, QA_W), _BF16),
        scratch_shapes=[pltpu.VMEM((VT_ROWS, per * tq), _F32)],
        compiler_params=_cparams(3),
        name="gqa_attn",
    )(qa, ka, vt)


class _Pattern:
    def __init__(self, window, dil, ni_q, unroll, shifts=()):
        self.dil = dil
        self.unroll = unroll
        self.shifts = shifts
        self.n_sub = max(len(shifts), 1)
        self.n_side = window // (2 * dil)
        self.n_rows = RES // dil
        self.ni_q = ni_q
        need = -(-self.n_side // self.n_rows)
        if shifts:
            self.halo = need
            assert ni_q * self.n_sub == BF16_ROWS and (ni_q + 2 * need) % BF16_ROWS == 0
            assert all((h * ni_q - need + s) % BF16_ROWS == 0 and need <= s <= ROW_PAD - need
                       for h, s in enumerate(shifts))
        else:
            self.halo = -(-need // BF16_ROWS) * BF16_ROWS
        self.ni_k = min(ni_q + 2 * self.halo, N_I)
        self.n_blk = N_I // ni_q
        self.variants = sorted({0, min(1, self.n_blk - 1), self.n_blk - 1})

    def key_start(self, b):
        i0 = b * self.ni_q
        return i0 - self.halo if self.shifts else min(max(i0 - self.halo, 0), N_I - self.ni_k)

    def _mask(self, b):
        a_q, i_q = np.divmod(np.arange(self.n_rows * self.ni_q), self.ni_q)
        a_k, i_k = np.divmod(np.arange(self.n_rows * self.ni_k), self.ni_k)
        i_q = i_q + b * self.ni_q
        i_k = i_k + self.key_start(b)
        dj = self.n_rows * (i_q[:, None] - i_k[None, :]) + (a_q[:, None] - a_k[None, :])
        ok = (np.abs(dj) <= self.n_side) & ((i_k >= 0) & (i_k < N_I))[None, :]
        return np.where(ok, 0.0, NEG_INF).astype(np.float32)

    def bias(self):
        for b in range(2, self.n_blk - 1):
            assert np.array_equal(self._mask(b), self._mask(1))
        return jnp.asarray(np.stack([self._mask(b) for b in self.variants]))


RES = 16
N_I = 4096 // RES
ROW_PAD = BF16_ROWS
N_I_PAD = N_I + ROW_PAD
KEY_SHIFTS = (4, 12)
PATTERNS = (_Pattern(128, 1, 8, 16, shifts=KEY_SHIFTS), _Pattern(512, 4, 32, 32),
            _Pattern(2048, 16, N_I, 16))
assert tuple((2 * p.n_side * p.dil, p.dil) for p in PATTERNS) == DILATED_PATTERNS


def _dil_kernel(q_ref, k_ref, v_ref, ks0_ref, vs0_ref, ks1_ref, vs1_ref,
                b0_ref, b1_ref, b2_ref, out_ref, a_s, m_s, l_s):
    bias_refs = (b0_ref, b1_ref, b2_ref)
    shifted_refs = ((ks0_ref, vs0_ref), (ks1_ref, vs1_ref))
    n_stored = len(PATTERNS) - 1
    assert PATTERNS[-1].n_rows == 1 and PATTERNS[-1].ni_q == N_I and not PATTERNS[-1].shifts

    def attend(q, kw, vw, bias):
        nq, nk = q.shape[0], kw.shape[0]
        low = lax.broadcasted_iota(jnp.int32, (nq, LANES), 1) < HEAD_DIM
        vext = jnp.concatenate([vw, jnp.ones((nk, LANES), _BF16)], axis=1)
        zero = jnp.zeros_like(q)
        q2 = jnp.concatenate([jnp.where(low, q, zero), jnp.where(low, zero, q)], axis=0)
        s = _dot_nt(q2, kw) + jnp.concatenate([bias, bias], axis=0)
        m = jnp.max(s, axis=-1, keepdims=True)
        pv = _dot(jnp.exp2(s - m).astype(_BF16), vext)
        mb = jnp.broadcast_to(m, (2 * nq, LANES))
        return tuple(jnp.where(low, t[0:nq], t[nq:])
                     for t in (pv[:, 0:LANES], mb, pv[:, LANES:2 * LANES]))

    for pi, pat in enumerate(PATTERNS):
        n_var = len(pat.variants)
        ni_load = pat.ni_q * pat.n_sub
        n_load = N_I // ni_load

        def body(blk, carry, pi=pi, pat=pat, n_var=n_var, ni_load=ni_load, n_load=n_load):
            rho = blk // n_load
            il = blk % n_load
            i0 = pl.multiple_of(il * ni_load, ni_load)
            rows = [rho + pat.dil * a for a in range(pat.n_rows)]
            qs = [q_ref[0, r, pl.ds(i0, ni_load), :] for r in rows]
            if pat.shifts:
                qs = [c.astype(_F32) for c in qs]
            for h in range(pat.n_sub):
                ib = il * pat.n_sub + h
                iq = pl.multiple_of(i0 + h * pat.ni_q, pat.ni_q)
                if pat.shifts:
                    kr, vr = shifted_refs[h]
                    ws = i0 + (h * pat.ni_q - pat.halo + pat.shifts[h])
                    q = jnp.concatenate(
                        [c[h * pat.ni_q:(h + 1) * pat.ni_q] for c in qs], axis=0).astype(_BF16)
                else:
                    kr, vr = k_ref, v_ref
                    ws = jnp.clip(i0 - pat.halo, 0, N_I - pat.ni_k)
                    q = jnp.concatenate(qs, axis=0)
                ws = pl.multiple_of(ws, BF16_ROWS)
                var = jnp.where(ib == 0, 0, jnp.where(ib == pat.n_blk - 1, n_var - 1, 1)) if n_var > 1 else 0
                kw = jnp.concatenate([kr[0, r, pl.ds(ws, pat.ni_k), :] for r in rows], axis=0)
                vw = jnp.concatenate([vr[0, r, pl.ds(ws, pat.ni_k), :] for r in rows], axis=0)
                acc, mx, den = attend(q, kw, vw, bias_refs[pi][var])
                if pi == n_stored:
                    parts = [(a_s[p, rho], m_s[p, rho], l_s[p, rho]) for p in range(n_stored)]
                    parts.append((acc, mx, den))
                    top = functools.reduce(jnp.maximum, [m for _, m, _ in parts])
                    es = [jnp.exp2(m - top) for _, m, _ in parts]
                    num = sum(e * a for e, (a, _, _) in zip(es, parts))
                    tot = sum(e * l for e, (_, _, l) in zip(es, parts))
                    out_ref[0, rho] = (num / tot).astype(out_ref.dtype)
                    continue
                for a, r in enumerate(rows):
                    sl = slice(a * pat.ni_q, (a + 1) * pat.ni_q)
                    a_s[pi, r, pl.ds(iq, pat.ni_q), :] = acc[sl]
                    m_s[pi, r, pl.ds(iq, pat.ni_q), :] = mx[sl]
                    l_s[pi, r, pl.ds(iq, pat.ni_q), :] = den[sl]
            return carry

        lax.fori_loop(0, pat.dil * n_load, body, 0, unroll=pat.unroll)


def _dilated(qb, kb, vb, shifted, biases):
    B, S, W = qb.shape
    view = lambda a: a.reshape(B, RES, N_I, W)
    spec = pl.BlockSpec((1, RES, N_I, LANES), lambda b, c: (b, 0, 0, c))
    sspec = pl.BlockSpec((1, RES, N_I_PAD, LANES), lambda b, c: (b, 0, 0, c))
    bspecs = [pl.BlockSpec(bb.shape, lambda b, c: (0, 0, 0)) for bb in biases]
    yb = pl.pallas_call(
        _dil_kernel,
        grid=(B, W // LANES),
        in_specs=[spec, spec, spec] + [sspec] * len(shifted) + bspecs,
        out_specs=spec,
        out_shape=jax.ShapeDtypeStruct((B, RES, N_I, W), _BF16),
        scratch_shapes=[pltpu.VMEM((len(PATTERNS) - 1, RES, N_I, LANES), _F32)] * 3,
        compiler_params=_cparams(2),
        name="dilated",
    )(view(qb), view(kb), view(vb), *shifted, *biases)
    return yb.reshape(B, S, W)


def _post_kernel(x_ref, ya_ref, yb_ref, ga_ref, gb_ref, wo_ref, n2_ref, w1_ref, w2_ref,
                 fn_ref, out_ref, *, chunk, final_norm):
    ya, yb = ya_ref[0].astype(_F32), yb_ref[0].astype(_F32)
    ya_n = (ya * _rms_scale(ya) * ga_ref[...]).astype(_BF16)
    yb_n = (yb * _rms_scale(yb) * gb_ref[...]).astype(_BF16)
    x = (x_ref[0] + _dot(ya_n, wo_ref[0:QA_W, :])
         + _dot(yb_n, wo_ref[QA_W:QA_W + QB_W, :]))
    h = (x * _rms_scale(x) * n2_ref[...]).astype(_BF16)
    acc = jnp.zeros_like(x)
    for c in range(D_FF // chunk):
        sl = slice(c * chunk, (c + 1) * chunk)
        u = jnp.maximum(_dot(h, w1_ref[:, sl]), 0.0)
        acc = acc + _dot((u * u).astype(_BF16), w2_ref[sl, :])
    acc = x + acc
    if final_norm:
        acc = acc * _rms_scale(acc) * fn_ref[...]
    out_ref[0] = acc


def _post(x, ya, yb, ga, gb, w_out, n2, w1, w2, layer, fn, tm, final_norm):
    B, S, D = x.shape
    const = lambda b, i: (0, 0)
    half = pl.BlockSpec((1, tm, QA_W), lambda b, i: (b, i, 0))
    full = pl.BlockSpec((1, tm, D), lambda b, i: (b, i, 0))
    return pl.pallas_call(
        functools.partial(_post_kernel, chunk=1024, final_norm=final_norm),
        grid=(B, S // tm),
        in_specs=[full, half, half,
                  pl.BlockSpec((1, QA_W), const), pl.BlockSpec((1, QB_W), const),
                  _layer_weight(w_out, layer),
                  pl.BlockSpec((1, D), const), _layer_weight(w1, layer), _layer_weight(w2, layer),
                  pl.BlockSpec((1, D), const)],
        out_specs=full,
        out_shape=jax.ShapeDtypeStruct((B, S, D), _F32),
        compiler_params=_cparams(2),
        name="post",
    )(x, ya, yb, ga, gb, w_out, n2, w1, w2, fn)


def kernel(x, norm1, w_in, q_norm, k_norm, out_norm_a, out_norm_b, w_out,
           norm2, w_mlp_in, w_mlp_out, final_norm):
    B, S, D = x.shape
    assert S == RES * N_I
    depth = w_in.shape[0]
    to_res = lambda a: jnp.swapaxes(a.reshape(a.shape[:-2] + (N_I, RES, a.shape[-1])), -3, -2
                                    ).reshape(a.shape)
    from_res = lambda a: jnp.swapaxes(a.reshape(a.shape[:-2] + (RES, N_I, a.shape[-1])), -3, -2
                                      ).reshape(a.shape)
    tables = [jnp.asarray(t.reshape(N_I, RES, LANES).swapaxes(0, 1).reshape(S, LANES))
              for t in _rope_tables(S)]
    biases = [p.bias() for p in PATTERNS]
    gm = _group_mean_matrix()
    row = lambda a: a.reshape(1, -1).astype(_F32)
    fn = row(final_norm)
    w_in, w_out, w_mlp_in, w_mlp_out = (w.astype(_BF16) for w in (w_in, w_out, w_mlp_in, w_mlp_out))
    x = to_res(x)

    def in_params(l):
        head_gain = lambda g: row(jnp.tile(g, LANES // HEAD_DIM))
        return (row(norm1[l]), w_in, head_gain(q_norm[l]), head_gain(k_norm[l]), gm, *tables)

    for l in range(depth):
        qa, ka, vt, qb, kb, vb, *shifted = _in_proj(x, in_params(l), w_in, l, tm=512)
        ya = _gqa(qa, ka, vt, tq=1024, tk=512)
        yb = _dilated(qb, kb, vb, shifted, biases)
        x = _post(x, ya, yb, row(out_norm_a[l]), row(out_norm_b[l]), w_out,
                  row(norm2[l]), w_mlp_in, w_mlp_out, l, fn, tm=512,
                  final_norm=(l == depth - 1))
    return from_res(x)
```
